```python
import math
import jax, jax.numpy as jnp
from jax import lax
import numpy as np

D_MODEL = 2048
BATCH = 2
SEQ = 16384
DEPTH = 1

F32 = jnp.float32
GRID_W = 64
CTX_LEN = 256
N_MOD = 6
NORM_EPS = 1e-6
MLA_HEADS = 16
Q_LORA = 512
KV_LORA = 512
QK_NOPE = 128
QK_ROPE = 64
V_HEAD = 128
ROPE_THETA = 10000.0
ATTN_SCALE = (QK_NOPE + QK_ROPE) ** -0.5
Q_BLOCK = 128
SSD_EXPAND = 2
SSD_INNER = SSD_EXPAND * D_MODEL
SSD_HEADDIM = 64
SSD_HEADS = SSD_INNER // SSD_HEADDIM
SSD_GROUPS = 8
SSD_STATE = 128
SSD_CONV = 5
SSD_CHUNK = 64
SSD_XBC = SSD_INNER + 2 * SSD_GROUPS * SSD_STATE
N_GROUPS_E = 8
EXPERTS_PER_GROUP = 8
N_EXPERTS = N_GROUPS_E * EXPERTS_PER_GROUP
TOP_K_IN_GROUP = 2
EXPERT_FF = 1024
MOE_BLOCK = 128
IN_WIDTH = Q_LORA + KV_LORA + QK_ROPE + SSD_INNER + SSD_XBC + 2 * SSD_HEADS + 2 * D_MODEL

kernel_name = 'hybrid_ssd_mla_hmoe_diffusion_block'


def rms_norm(x, g):
    xf = x.astype(F32)
    y = xf * lax.rsqrt(jnp.mean(xf * xf, axis=-1, keepdims=True) + NORM_EPS)
    return (y * g.astype(F32)).astype(x.dtype)


def modulate(x, g, shift, scale):
    return rms_norm(x, g) * (1 + scale) + shift


def split_in_proj(u):
    widths = (Q_LORA, KV_LORA, QK_ROPE, SSD_INNER, SSD_XBC, 2 * SSD_HEADS, 2 * D_MODEL)
    offsets = [sum(widths[:i + 1]) for i in range(len(widths) - 1)]
    return jnp.split(u, offsets, axis=-1)


def axial_rope_tables(n_tokens):
    rows = n_tokens // GRID_W
    row = jnp.repeat(jnp.arange(rows), GRID_W).astype(F32)
    col = jnp.tile(jnp.arange(GRID_W), rows).astype(F32)
    half = QK_ROPE // 2
    freqs = ROPE_THETA ** (-jnp.arange(0, half, 2, dtype=F32) / half)
    ang_r = row[:, None] * freqs
    ang_c = col[:, None] * freqs
    return (jnp.cos(ang_r), jnp.sin(ang_r), jnp.cos(ang_c), jnp.sin(ang_c))


def rotate_half_rope(x, cos, sin):
    k = x.shape[-1] // 2
    x1, x2 = x[..., :k].astype(F32), x[..., k:].astype(F32)
    cos, sin = cos[:, None, :], sin[:, None, :]
    return jnp.concatenate([x1 * cos - x2 * sin, x1 * sin + x2 * cos], axis=-1).astype(x.dtype)


def rope_2d(x, tables):
    cos_r, sin_r, cos_c, sin_c = tables
    half = QK_ROPE // 2
    return jnp.concatenate([rotate_half_rope(x[..., :half], cos_r, sin_r),
                            rotate_half_rope(x[..., half:], cos_c, sin_c)], axis=-1)


def mla_q(cq, lp, tables):
    b, n, _ = cq.shape
    q = (rms_norm(cq, lp['g_q']) @ lp['w_uq']).reshape(b, n, MLA_HEADS, QK_NOPE + QK_ROPE)
    q_pe = q[..., QK_NOPE:]
    if tables is not None:
        q_pe = rope_2d(q_pe, tables)
    return jnp.concatenate([q[..., :QK_NOPE], q_pe], axis=-1)


def mla_kv(ckv, kr, lp, tables):
    b, n, _ = ckv.shape
    kv = (rms_norm(ckv, lp['g_kv']) @ lp['w_ukv']).reshape(b, n, MLA_HEADS, QK_NOPE + V_HEAD)
    k_pe = kr[:, :, None, :]
    if tables is not None:
        k_pe = rope_2d(k_pe, tables)
    k_pe = jnp.broadcast_to(k_pe, (b, n, MLA_HEADS, QK_ROPE))
    k = jnp.concatenate([kv[..., :QK_NOPE], k_pe], axis=-1)
    return k, kv[..., QK_NOPE:]


def attend(q, k, v):
    s = jnp.einsum('bqhd,bkhd->bhqk', q, k).astype(F32) * ATTN_SCALE
    p = jax.nn.softmax(s, axis=-1).astype(v.dtype)
    return jnp.einsum('bhqk,bkhd->bqhd', p, v)


def blocked_attention(q, k, v):
    b, n, h, d = q.shape
    nb = n // Q_BLOCK
    qb = q.reshape(b, nb, Q_BLOCK, h, d).transpose(1, 0, 2, 3, 4)
    ob = lax.map(lambda qi: attend(qi, k, v), qb)
    return ob.transpose(1, 0, 2, 3, 4).reshape(b, n, h, V_HEAD)


def depthwise_conv(u, w, bias):
    c = u.shape[-1]
    y = lax.conv_general_dilated(u, w[:, None, :].astype(u.dtype), window_strides=(1,),
                                 padding=[(SSD_CONV // 2, SSD_CONV // 2)],
                                 dimension_numbers=('NWC', 'WIO', 'NWC'), feature_group_count=c)
    return y + bias


def ssd_prep(xbc_raw, dt_raw, lp):
    xbc = jax.nn.silu(depthwise_conv(xbc_raw, lp['conv_w'], lp['conv_b']))
    b, n, _ = xbc.shape
    gn = SSD_GROUPS * SSD_STATE
    xs = xbc[..., :SSD_INNER].reshape(b, n, SSD_HEADS, SSD_HEADDIM)
    bm = xbc[..., SSD_INNER:SSD_INNER + gn].reshape(b, n, SSD_GROUPS, SSD_STATE)
    cm = xbc[..., SSD_INNER + gn:].reshape(b, n, SSD_GROUPS, SSD_STATE)
    dt = jax.nn.softplus(dt_raw.astype(F32).reshape(b, n, 2, SSD_HEADS) + lp['dt_bias'].astype(F32))
    return xs, bm, cm, dt


def ssd_chunked(x, dt, a_coef, bm, cm, init_state):
    b, n, h, p = x.shape
    g, s_dim = bm.shape[2], bm.shape[3]
    r = h // g
    q = SSD_CHUNK
    nc = n // q
    a = (dt * a_coef).reshape(b, nc, q, g, r).transpose(0, 3, 4, 1, 2)
    a_cs = jnp.cumsum(a, axis=-1)
    xdt = (x.astype(F32) * dt[..., None]).reshape(b, nc, q, g, r, p)
    bc = bm.astype(F32).reshape(b, nc, q, g, s_dim)
    cc = cm.astype(F32).reshape(b, nc, q, g, s_dim)
    lower = jnp.tril(jnp.ones((q, q), dtype=bool))
    seg = a_cs[..., :, None] - a_cs[..., None, :]
    decay_in = jnp.exp(jnp.where(lower, seg, -jnp.inf))
    cb = jnp.einsum('bclgn,bcsgn->bgcls', cc, bc)
    y_diag = jnp.einsum('bgcls,bgrcls,bcsgrp->bclgrp', cb, decay_in, xdt)
    decay_states = jnp.exp(a_cs[..., -1:] - a_cs)
    states = jnp.einsum('bcsgn,bgrcs,bcsgrp->cbgrpn', bc, decay_states, xdt)
    chunk_decay = jnp.exp(a_cs[..., -1]).transpose(3, 0, 1, 2)

    def step(carry, inp):
        st, dec = inp
        return carry * dec[..., None, None] + st, carry

    final, prev = lax.scan(step, init_state.astype(F32).reshape(b, g, r, p, s_dim), (states, chunk_decay))
    y_off = jnp.einsum('bclgn,cbgrpn,bgrcl->bclgrp', cc, prev, jnp.exp(a_cs))
    y = (y_diag + y_off).reshape(b, n, h, p)
    return y, final.reshape(b, h, p, s_dim)


def ssd_final_state(x, dt, a_coef, bm):
    b, n, h, p = x.shape
    g, s_dim = bm.shape[2], bm.shape[3]
    r = h // g
    cs = jnp.cumsum(dt * a_coef, axis=1)
    w = jnp.exp(cs[:, -1:] - cs) * dt
    st = jnp.einsum('blgn,blgr,blgrp->bgrpn', bm.astype(F32), w.reshape(b, n, g, r),
                    x.astype(F32).reshape(b, n, g, r, p))
    return st.reshape(b, h, p, s_dim)


def ssd_gate_out(y, z, lp):
    b, n = z.shape[:2]
    y = y.reshape(b, n, SSD_INNER) * jax.nn.silu(z.astype(F32))
    return rms_norm(y, lp['g_ssd']).astype(z.dtype) @ lp['w_ssd_out']


def ssd_mixer(z_l, xbc_l, dt_l, z_c, xbc_c, dt_c, lp, update_ctx):
    x_l, b_l, c_l, dts_l = ssd_prep(xbc_l, dt_l, lp)
    x_c, b_c, c_c, dts_c = ssd_prep(xbc_c, dt_c, lp)
    a_coef = -jnp.exp(lp['a_log'].astype(F32))
    d_skip = lp['d_skip'].astype(F32)[:, None]
    y_l = d_skip * x_l.astype(F32)
    y_c = d_skip * x_c.astype(F32) if update_ctx else None
    bsz = x_c.shape[0]
    for d in range(2):
        o = (lambda t: t) if d == 0 else (lambda t: jnp.flip(t, axis=1))
        if update_ctx:
            zero = jnp.zeros((bsz, SSD_HEADS, SSD_HEADDIM, SSD_STATE), F32)
            yc, s0 = ssd_chunked(o(x_c), o(dts_c[:, :, d]), a_coef[d], o(b_c), o(c_c), zero)
            y_c = y_c + o(yc)
        else:
            s0 = ssd_final_state(o(x_c), o(dts_c[:, :, d]), a_coef[d], o(b_c))
        yl, _ = ssd_chunked(o(x_l), o(dts_l[:, :, d]), a_coef[d], o(b_l), o(c_l), s0)
        y_l = y_l + o(yl)
    out_l = ssd_gate_out(y_l, z_l, lp)
    out_c = ssd_gate_out(y_c, z_c, lp) if update_ctx else None
    return out_l, out_c


def merge_branches(y_ssd, o_mla, gates, lp):
    b, n = o_mla.shape[:2]
    y_mla = o_mla.reshape(b, n, MLA_HEADS * V_HEAD) @ lp['w_o_mla']
    g_ssd, g_mla = gates[..., :D_MODEL], gates[..., D_MODEL:]
    return (jax.nn.sigmoid(g_ssd) * y_ssd + jax.nn.sigmoid(g_mla) * y_mla) @ lp['w_out']


def token_mixer(h_l, h_c, lp, tables, update_ctx):
    cq_l, ckv_l, kr_l, z_l, xbc_l, dt_l, gate_l = split_in_proj(h_l @ lp['w_in'])
    cq_c, ckv_c, kr_c, z_c, xbc_c, dt_c, gate_c = split_in_proj(h_c @ lp['w_in'])
    k_l, v_l = mla_kv(ckv_l, kr_l, lp, tables)
    k_c, v_c = mla_kv(ckv_c, kr_c, lp, None)
    o_l = blocked_attention(mla_q(cq_l, lp, tables),
                            jnp.concatenate([k_l, k_c], axis=1), jnp.concatenate([v_l, v_c], axis=1))
    s_l, s_c = ssd_mixer(z_l, xbc_l, dt_l, z_c, xbc_c, dt_c, lp, update_ctx)
    out_l = merge_branches(s_l, o_l, gate_l, lp)
    out_c = merge_branches(s_c, attend(mla_q(cq_c, lp, None), k_c, v_c), gate_c, lp) if update_ctx else None
    return out_l, out_c


def hier_moe(h, lp):
    t, d = h.shape
    g_prob = jax.nn.softmax((h @ lp['w_router_group']).astype(F32) + lp['b_router_group'], axis=-1)
    g_top, g_sel = lax.top_k(g_prob, 1)
    e_logits = ((h @ lp['w_router_expert']).astype(F32) + lp['b_router_expert']).reshape(t, N_GROUPS_E, EXPERTS_PER_GROUP)
    e_logits = jnp.take_along_axis(e_logits, g_sel[:, :, None], axis=1)[:, 0]
    e_top, e_sel = lax.top_k(jax.nn.softmax(e_logits, axis=-1), TOP_K_IN_GROUP)
    weights = g_top * e_top / jnp.sum(e_top, axis=-1, keepdims=True)
    expert_id = g_sel * EXPERTS_PER_GROUP + e_sel
    n_assign = t * TOP_K_IN_GROUP
    eid = expert_id.reshape(n_assign)
    tid = jnp.repeat(jnp.arange(t, dtype=jnp.int32), TOP_K_IN_GROUP)
    wts = weights.reshape(n_assign)
    order = jnp.argsort(eid)
    eid_s, tid_s, w_s = eid[order], tid[order], wts[order]
    counts = jnp.bincount(eid, length=N_EXPERTS)
    start = jnp.cumsum(counts) - counts
    padded = (counts + MOE_BLOCK - 1) // MOE_BLOCK * MOE_BLOCK
    pend = jnp.cumsum(padded)
    pstart = pend - padded
    dest = pstart[eid_s] + (jnp.arange(n_assign) - start[eid_s])
    n_blocks = (n_assign + MOE_BLOCK - 1) // MOE_BLOCK + N_EXPERTS
    rows = n_blocks * MOE_BLOCK
    x_buf = jnp.zeros((rows, d), h.dtype).at[dest].set(h[tid_s])
    w_buf = jnp.zeros((rows,), F32).at[dest].set(w_s)
    t_buf = jnp.zeros((rows,), jnp.int32).at[dest].set(tid_s)
    blk_e = jnp.clip(jnp.searchsorted(pend, jnp.arange(n_blocks) * MOE_BLOCK, side='right'), 0, N_EXPERTS - 1)

    def run_block(args):
        xb, e = args
        return (jax.nn.silu(xb @ lp['w_exp_gate'][e]) * (xb @ lp['w_exp_up'][e])) @ lp['w_exp_down'][e]

    y_buf = lax.map(run_block, (x_buf.reshape(n_blocks, MOE_BLOCK, d), blk_e)).reshape(rows, d)
    return jnp.zeros((t, d), h.dtype).at[t_buf].add(y_buf * w_buf[:, None].astype(h.dtype))


def hybrid_layer(x_l, x_c, c, c_ctx, lp, tables, update_ctx):
    mod_l = (jax.nn.silu(c) @ lp['w_ada'] + lp['b_ada'])[:, None, :]
    mod_c = jax.nn.silu(c_ctx) @ lp['w_ada'] + lp['b_ada']
    sh_a, sc_a, g_a, sh_f, sc_f, g_f = jnp.split(mod_l, N_MOD, axis=-1)
    csh_a, csc_a, cg_a, csh_f, csc_f, cg_f = jnp.split(mod_c, N_MOD, axis=-1)
    y_l, y_c = token_mixer(modulate(x_l, lp['g_pre_mix'], sh_a, sc_a),
                           modulate(x_c, lp['g_pre_mix'], csh_a, csc_a), lp, tables, update_ctx)
    x_l = x_l + g_a * rms_norm(y_l, lp['g_post_mix'])
    h_l = modulate(x_l, lp['g_pre_ffn'], sh_f, sc_f)
    b, n, d = h_l.shape
    if update_ctx:
        x_c = x_c + cg_a * rms_norm(y_c, lp['g_post_mix'])
        h_c = modulate(x_c, lp['g_pre_ffn'], csh_f, csc_f)
        f = hier_moe(jnp.concatenate([h_l.reshape(b * n, d), h_c.reshape(-1, d)], axis=0), lp)
        f_l = f[:b * n].reshape(b, n, d)
        x_c = x_c + cg_f * rms_norm(f[b * n:].reshape(x_c.shape), lp['g_post_ffn'])
    else:
        f_l = hier_moe(h_l.reshape(b * n, d), lp).reshape(b, n, d)
    x_l = x_l + g_f * rms_norm(f_l, lp['g_post_ffn'])
    return x_l, x_c


def setup_inputs(seed: int = 0) -> dict:
    key = jax.random.key(seed)
    ks = jax.random.split(key, 32)
    L, D, H = DEPTH, D_MODEL, SSD_HEADS

    def nrm(k, shape, scale):
        return jax.random.normal(k, shape, F32) * scale

    dt0 = jnp.exp(jax.random.uniform(ks[19], (L, 2, H), F32, math.log(1e-3), math.log(1e-1)))
    return {
        'x': nrm(ks[0], (BATCH, SEQ, D), 1.0),
        'c': nrm(ks[1], (BATCH, D), 1.0),
        'ctx': nrm(ks[2], (BATCH, CTX_LEN, D), 1.0),
        'c_ctx': nrm(ks[3], (D,), 1.0),
        'w_ada': nrm(ks[4], (L, D, N_MOD * D), 0.5 * D ** -0.5),
        'b_ada': nrm(ks[5], (L, N_MOD * D), 0.02),
        'g_pre_mix': 1.0 + nrm(ks[6], (L, D), 0.1),
        'g_post_mix': 1.0 + nrm(ks[7], (L, D), 0.1),
        'g_pre_ffn': 1.0 + nrm(ks[8], (L, D), 0.1),
        'g_post_ffn': 1.0 + nrm(ks[9], (L, D), 0.1),
        'w_in': nrm(ks[10], (L, D, IN_WIDTH), D ** -0.5),
        'g_q': 1.0 + nrm(ks[11], (L, Q_LORA), 0.1),
        'w_uq': nrm(ks[12], (L, Q_LORA, MLA_HEADS * (QK_NOPE + QK_ROPE)), Q_LORA ** -0.5),
        'g_kv': 1.0 + nrm(ks[13], (L, KV_LORA), 0.1),
        'w_ukv': nrm(ks[14], (L, KV_LORA, MLA_HEADS * (QK_NOPE + V_HEAD)), KV_LORA ** -0.5),
        'w_o_mla': nrm(ks[15], (L, MLA_HEADS * V_HEAD, D), (MLA_HEADS * V_HEAD) ** -0.5),
        'conv_w': nrm(ks[16], (L, SSD_CONV, SSD_XBC), SSD_CONV ** -0.5),
        'conv_b': nrm(ks[17], (L, SSD_XBC), 0.02),
        'a_log': jnp.log(jax.random.uniform(ks[18], (L, 2, H), F32, 1.0, 16.0)),
        'dt_bias': dt0 + jnp.log(-jnp.expm1(-dt0)),
        'd_skip': 1.0 + nrm(ks[20], (L, H), 0.1),
        'g_ssd': 1.0 + nrm(ks[21], (L, SSD_INNER), 0.1),
        'w_ssd_out': nrm(ks[22], (L, SSD_INNER, D), SSD_INNER ** -0.5),
        'w_out': nrm(ks[23], (L, D, D), D ** -0.5),
        'w_router_group': nrm(ks[24], (L, D, N_GROUPS_E), D ** -0.5),
        'b_router_group': nrm(ks[25], (L, N_GROUPS_E), 0.01),
        'w_router_expert': nrm(ks[26], (L, D, N_EXPERTS), D ** -0.5),
        'b_router_expert': nrm(ks[27], (L, N_EXPERTS), 0.01),
        'w_exp_gate': nrm(ks[28], (L, N_EXPERTS, D, EXPERT_FF), D ** -0.5),
        'w_exp_up': nrm(ks[29], (L, N_EXPERTS, D, EXPERT_FF), D ** -0.5),
        'w_exp_down': nrm(ks[30], (L, N_EXPERTS, EXPERT_FF, D), EXPERT_FF ** -0.5),
    }


def reference(x, c, ctx, c_ctx, w_ada, b_ada, g_pre_mix, g_post_mix, g_pre_ffn, g_post_ffn, w_in, g_q, w_uq,
              g_kv, w_ukv, w_o_mla, conv_w, conv_b, a_log, dt_bias, d_skip, g_ssd, w_ssd_out, w_out,
              w_router_group, b_router_group, w_router_expert, b_router_expert, w_exp_gate, w_exp_up, w_exp_down):
    tables = axial_rope_tables(x.shape[1])
    x_l, x_c = x, ctx
    for i in range(DEPTH):
        lp = dict(w_ada=w_ada[i], b_ada=b_ada[i], g_pre_mix=g_pre_mix[i], g_post_mix=g_post_mix[i],
                  g_pre_ffn=g_pre_ffn[i], g_post_ffn=g_post_ffn[i], w_in=w_in[i], g_q=g_q[i], w_uq=w_uq[i],
                  g_kv=g_kv[i], w_ukv=w_ukv[i], w_o_mla=w_o_mla[i], conv_w=conv_w[i], conv_b=conv_b[i],
                  a_log=a_log[i], dt_bias=dt_bias[i], d_skip=d_skip[i], g_ssd=g_ssd[i], w_ssd_out=w_ssd_out[i],
                  w_out=w_out[i], w_router_group=w_router_group[i], b_router_group=b_router_group[i],
                  w_router_expert=w_router_expert[i], b_router_expert=b_router_expert[i],
                  w_exp_gate=w_exp_gate[i], w_exp_up=w_exp_up[i], w_exp_down=w_exp_down[i])
        x_l, x_c = hybrid_layer(x_l, x_c, c, c_ctx, lp, tables, i < DEPTH - 1)
    return x_l
```

```python
import functools
import math

import jax
import jax.numpy as jnp
from jax import lax
from jax.experimental import pallas as pl
from jax.experimental.pallas import tpu as pltpu

F32 = jnp.float32
BF16 = jnp.bfloat16
U32 = jnp.uint32

GRID_W = 64
N_MOD = 6
NORM_EPS = 1e-6
MLA_HEADS = 16
Q_LORA = 512
KV_LORA = 512
QK_NOPE = 128
QK_ROPE = 64
QK_DIM = QK_NOPE + QK_ROPE
V_HEAD = 128
ROPE_THETA = 10000.0
ATTN_SCALE = QK_DIM ** -0.5
SSD_HEADDIM = 64
SSD_HEADS = 64
SSD_INNER = SSD_HEADS * SSD_HEADDIM
SSD_GROUPS = 8
SSD_STATE = 128
SSD_CONV = 5
HEADS_PER_GROUP = SSD_HEADS // SSD_GROUPS
GROUP_WIDTH = HEADS_PER_GROUP * SSD_HEADDIM
N_GROUPS_E = 8
EXPERTS_PER_GROUP = 8
N_EXPERTS = N_GROUPS_E * EXPERTS_PER_GROUP
TOP_K = 2
LOG2E = 1.4426950408889634

U_Z = 0
U_GATE = U_Z + SSD_INNER
U_XBC = U_GATE + 2 * 2048
U_CQ = U_XBC + SSD_INNER + 2 * SSD_GROUPS * SSD_STATE
U_CKV = U_CQ + Q_LORA
U_MAIN = U_CKV + KV_LORA
U_SMALL = 256

SSD_CHUNK = 128
MOE_BLOCK = 256
VMEM_LIMIT = 56 * 1024 * 1024


def _cparams(sem):
    return pltpu.CompilerParams(dimension_semantics=sem, vmem_limit_bytes=VMEM_LIMIT)


def _silu(x):
    return x * jax.nn.sigmoid(x)


def _rms(x, eps=NORM_EPS):
    return x * lax.rsqrt(jnp.mean(x * x, axis=-1, keepdims=True) + eps)


def _ada_kernel(c_ref, w_ref, b_ref, o_ref):
    c = c_ref[...]
    o_ref[...] = jnp.dot(_silu(c).astype(BF16), w_ref[...].astype(BF16),
                         preferred_element_type=F32) + b_ref[...]


def _ada(c8, w_ada, b_ada):
    d, n = w_ada.shape
    tn = 1536
    return pl.pallas_call(
        _ada_kernel,
        grid=(n // tn,),
        in_specs=[pl.BlockSpec((8, d), lambda j: (0, 0)),
                  pl.BlockSpec((d, tn), lambda j: (0, j)),
                  pl.BlockSpec((1, tn), lambda j: (0, j))],
        out_specs=pl.BlockSpec((8, tn), lambda j: (0, j)),
        out_shape=jax.ShapeDtypeStruct((8, n), F32),
        compiler_params=_cparams(("parallel",)),
        name="ada",
    )(c8, w_ada, b_ada.reshape(1, n))


def _inproj_kernel(x_ref, sc_ref, sh_ref, g_ref, wm_ref, ws_ref, om_ref, os_ref, h_scr):
    @pl.when(pl.program_id(2) == 0)
    def _():
        y = _rms(x_ref[...]) * (g_ref[...] * (1.0 + sc_ref[...])) + sh_ref[...]
        hb = y.astype(BF16)
        h_scr[...] = hb
        os_ref[...] = jnp.dot(hb, ws_ref[...], preferred_element_type=F32)

    om_ref[...] = jnp.dot(h_scr[...], wm_ref[...], preferred_element_type=F32).astype(BF16)


def _inproj(x, scale, shift, g, w_main, w_small, tm):
    b, n, d = x.shape
    tn = 1536
    return pl.pallas_call(
        _inproj_kernel,
        grid=(b, n // tm, U_MAIN // tn),
        in_specs=[pl.BlockSpec((None, tm, d), lambda bi, i, j: (bi, i, 0)),
                  pl.BlockSpec((None, 1, d), lambda bi, i, j: (bi, 0, 0)),
                  pl.BlockSpec((None, 1, d), lambda bi, i, j: (bi, 0, 0)),
                  pl.BlockSpec((1, d), lambda bi, i, j: (0, 0)),
                  pl.BlockSpec((d, tn), lambda bi, i, j: (0, j)),
                  pl.BlockSpec((d, U_SMALL), lambda bi, i, j: (0, 0))],
        out_specs=[pl.BlockSpec((None, tm, tn), lambda bi, i, j: (bi, i, j)),
                   pl.BlockSpec((None, tm, U_SMALL), lambda bi, i, j: (bi, i, 0))],
        out_shape=[jax.ShapeDtypeStruct((b, n, U_MAIN), BF16),
                   jax.ShapeDtypeStruct((b, n, U_SMALL), F32)],
        scratch_shapes=[pltpu.VMEM((tm, d), BF16)],
        compiler_params=_cparams(("parallel", "parallel", "arbitrary")),
        name="in_proj",
    )(x, scale, shift, g, w_main, w_small)


_NT = (((1,), (1,)), ((), ()))


def _qkv_kernel(*refs, rope, with_q):
    if rope:
        (cq_ref, ckv_ref, kr_ref, gq_ref, gkv_ref, wq_ref, wk_ref, wv_ref,
         cost_ref, sint_ref, cos_ref, sin_ref, q_ref, k_ref, v_ref) = refs
    else:
        ckv_ref, kr_ref, gkv_ref, wk_ref, wv_ref, k_ref, v_ref = refs
    ckvn = (_rms(ckv_ref[...].astype(F32)) * gkv_ref[...]).astype(BF16)
    kr = kr_ref[...]
    kpe = kr[:, :QK_ROPE]
    if rope:
        kpe = kpe * cos_ref[...] + kr[:, QK_ROPE:] * sin_ref[...]
    kpe = kpe.astype(BF16)
    if with_q:
        cqn = (_rms(cq_ref[...].astype(F32)) * gq_ref[...]).astype(BF16)
        cost = cost_ref[...]
        sint = sint_ref[...]

    def head(h, carry):
        if with_q:
            qt = lax.dot_general(wq_ref[h], cqn, _NT, preferred_element_type=F32)
            qp = qt[QK_NOPE:]
            qsw = jnp.concatenate([qp[16:32], qp[0:16], qp[48:64], qp[32:48]], axis=0)
            qp = qp * cost + qsw * sint
            q_ref[h, :QK_NOPE, :] = (qt[:QK_NOPE] * (ATTN_SCALE * LOG2E)).astype(BF16)
            q_ref[h, QK_NOPE:, :] = (qp * (ATTN_SCALE * LOG2E)).astype(BF16)
        kn = jnp.dot(ckvn, wk_ref[h], preferred_element_type=F32)
        k_ref[h, :, :QK_NOPE] = kn.astype(BF16)
        k_ref[h, :, QK_NOPE:] = kpe
        vt = lax.dot_general(wv_ref[h], ckvn, _NT, preferred_element_type=F32)
        v_ref[h] = vt.astype(BF16)
        return carry

    lax.fori_loop(0, MLA_HEADS, head, 0)


def _qkv(u_main, u_small, g_q, g_kv, wq_t, wk, wv_t, tables, tn):
    b, n, _ = u_main.shape
    nt = n // tn
    h = MLA_HEADS
    rope = tables is not None
    in_specs = []
    args = []
    if rope:
        in_specs.append(pl.BlockSpec((None, tn, Q_LORA), lambda bi, i: (bi, i, U_CQ // Q_LORA)))
        args.append(u_main)
    in_specs += [pl.BlockSpec((None, tn, KV_LORA), lambda bi, i: (bi, i, U_CKV // KV_LORA)),
                 pl.BlockSpec((None, tn, 128), lambda bi, i: (bi, i, 1))]
    args += [u_main, u_small]
    if rope:
        in_specs.append(pl.BlockSpec((1, Q_LORA), lambda bi, i: (0, 0)))
        args.append(g_q)
    in_specs.append(pl.BlockSpec((1, KV_LORA), lambda bi, i: (0, 0)))
    args.append(g_kv)
    if rope:
        in_specs.append(pl.BlockSpec((h, QK_DIM, Q_LORA), lambda bi, i: (0, 0, 0)))
        args.append(wq_t)
    in_specs += [pl.BlockSpec((h, KV_LORA, QK_NOPE), lambda bi, i: (0, 0, 0)),
                 pl.BlockSpec((h, V_HEAD, KV_LORA), lambda bi, i: (0, 0, 0))]
    args += [wk, wv_t]
    out_specs = []
    out_shape = []
    if rope:
        cos_t, sin_t, cos, sin = tables
        in_specs += [pl.BlockSpec((QK_ROPE, tn), lambda bi, i: (0, i)),
                     pl.BlockSpec((QK_ROPE, tn), lambda bi, i: (0, i)),
                     pl.BlockSpec((tn, QK_ROPE), lambda bi, i: (i, 0)),
                     pl.BlockSpec((tn, QK_ROPE), lambda bi, i: (i, 0))]
        args += [cos_t, sin_t, cos, sin]
        out_specs.append(pl.BlockSpec((None, None, h, QK_DIM, tn), lambda bi, i: (bi, i, 0, 0, 0)))
        out_shape.append(jax.ShapeDtypeStruct((b, nt, h, QK_DIM, tn), BF16))
    out_specs += [pl.BlockSpec((None, None, h, tn, QK_DIM), lambda bi, i: (bi, i, 0, 0, 0)),
                  pl.BlockSpec((None, None, h, V_HEAD, tn), lambda bi, i: (bi, i, 0, 0, 0))]
    out_shape += [jax.ShapeDtypeStruct((b, nt, h, tn, QK_DIM), BF16),
                  jax.ShapeDtypeStruct((b, nt, h, V_HEAD, tn), BF16)]
    return pl.pallas_call(
        functools.partial(_qkv_kernel, rope=rope, with_q=rope),
        grid=(b, nt),
        in_specs=in_specs,
        out_specs=out_specs,
        out_shape=out_shape,
        compiler_params=_cparams(("parallel", "parallel")),
        name="qkv_rope" if rope else "kv_ctx",
    )(*args)


def _attn_kernel(q_ref, kl_ref, kc_ref, vl_ref, vc_ref, o_ref):
    qt = q_ref[...]
    tq = qt.shape[1]

    def step(kb, vb, carry):
        m, l, acc = carry
        s = jnp.dot(kb, qt, preferred_element_type=F32)
        m_new = jnp.maximum(m, jnp.max(s, axis=0, keepdims=True))
        alpha = jnp.exp2(m - m_new)
        p = jnp.exp2(s - m_new)
        l = alpha * l + jnp.sum(p, axis=0, keepdims=True)
        acc = alpha * acc + jnp.dot(vb, p.astype(BF16), preferred_element_type=F32)
        return m_new, l, acc

    init = (jnp.full((1, tq), -jnp.inf, F32), jnp.zeros((1, tq), F32), jnp.zeros((V_HEAD, tq), F32))
    carry = step(kc_ref[0], vc_ref[0], init)
    carry = lax.fori_loop(0, kl_ref.shape[0], lambda i, c: step(kl_ref[i], vl_ref[i], c), carry)
    _, l, acc = carry
    o_ref[...] = jnp.transpose(acc / l).astype(BF16)


def _attention(q_t, k_l, v_l, k_c, v_c, tq):
    b, nt, h, _, tn = q_t.shape
    n = nt * tn
    per = tn // tq
    tk = k_l.shape[3]
    tkc = k_c.shape[3]
    return pl.pallas_call(
        _attn_kernel,
        grid=(b, h, n // tq),
        in_specs=[pl.BlockSpec((None, None, None, QK_DIM, tq), lambda bi, hi, qi: (bi, qi // per, hi, 0, qi % per)),
                  pl.BlockSpec((None, nt, None, tk, QK_DIM), lambda bi, hi, qi: (bi, 0, hi, 0, 0)),
                  pl.BlockSpec((None, 1, None, tkc, QK_DIM), lambda bi, hi, qi: (bi, 0, hi, 0, 0)),
                  pl.BlockSpec((None, nt, None, V_HEAD, tk), lambda bi, hi, qi: (bi, 0, hi, 0, 0)),
                  pl.BlockSpec((None, 1, None, V_HEAD, tkc), lambda bi, hi, qi: (bi, 0, hi, 0, 0))],
        out_specs=pl.BlockSpec((None, tq, V_HEAD), lambda bi, hi, qi: (bi, qi, hi)),
        out_shape=jax.ShapeDtypeStruct((b, n, h * V_HEAD), BF16),
        compiler_params=_cparams(("parallel", "parallel", "arbitrary")),
        name="attention",
    )(q_t, k_l, k_c, v_l, v_c)


_HALO = 16


def _conv_kernel(p_ref, x_ref, n_ref, w_ref, b_ref, o_ref):
    i = pl.program_id(1)
    x = x_ref[...].astype(F32)
    t = x.shape[0]
    prev = jnp.where(i > 0, p_ref[...].astype(F32), 0.0)
    nxt = jnp.where(i < pl.num_programs(1) - 1, n_ref[...].astype(F32), 0.0)
    full = jnp.concatenate([prev, x, nxt], axis=0)
    w = w_ref[...]
    acc = b_ref[...] + w[0:1] * full[_HALO - 2:_HALO - 2 + t]
    for k in range(1, SSD_CONV):
        acc = acc + w[k:k + 1] * full[_HALO - 2 + k:_HALO - 2 + k + t]
    o_ref[...] = _silu(acc).astype(BF16)


def _conv(u_main, conv_w, conv_b, tc):
    b, n, _ = u_main.shape
    nh = n // _HALO
    per = tc // _HALO

    def run(width, col0, w, bias, name):
        nblk = w.shape[1] // width
        c0 = col0 // width
        return pl.pallas_call(
            _conv_kernel,
            grid=(b, n // tc, nblk),
            in_specs=[pl.BlockSpec((None, _HALO, width), lambda bi, i, j: (bi, jnp.maximum(i * per - 1, 0), c0 + j)),
                      pl.BlockSpec((None, tc, width), lambda bi, i, j: (bi, i, c0 + j)),
                      pl.BlockSpec((None, _HALO, width), lambda bi, i, j: (bi, jnp.minimum((i + 1) * per, nh - 1), c0 + j)),
                      pl.BlockSpec((SSD_CONV, width), lambda bi, i, j: (0, j)),
                      pl.BlockSpec((1, width), lambda bi, i, j: (0, j))],
            out_specs=pl.BlockSpec((None, None, tc, width), lambda bi, i, j: (bi, j, i, 0)),
            out_shape=jax.ShapeDtypeStruct((b, nblk, n, width), BF16),
            compiler_params=_cparams(("parallel", "arbitrary", "arbitrary")),
            name=name,
        )(u_main, u_main, u_main, w, bias)

    gn = SSD_GROUPS * SSD_STATE
    bias = conv_b.reshape(1, -1)
    xg = run(GROUP_WIDTH, U_XBC, conv_w[:, :SSD_INNER], bias[:, :SSD_INNER], "conv_x")
    bg = run(SSD_STATE, U_XBC + SSD_INNER, conv_w[:, SSD_INNER:SSD_INNER + gn], bias[:, SSD_INNER:SSD_INNER + gn], "conv_b")
    cg = run(SSD_STATE, U_XBC + SSD_INNER + gn, conv_w[:, SSD_INNER + gn:], bias[:, SSD_INNER + gn:], "conv_c")
    return xg, bg, cg


def _dt_kernel(dt_ref, bias_ref, a_ref, dtb_ref, dob_ref, dsb_ref, cst_ref, dc_ref):
    q = dt_ref.shape[0]
    dt = jax.nn.softplus(dt_ref[...] + bias_ref[...])
    a = dt * a_ref[...]
    row = lax.broadcasted_iota(jnp.int32, a.shape, 0)
    cs = a
    s = 1
    while s < q:
        cs = cs + jnp.where(row >= s, pltpu.roll(cs, s, 0), 0.0)
        s *= 2
    total = cs[q - 1:q, :]
    lane = lax.broadcasted_iota(jnp.int32, a.shape, 1)
    cs = jnp.where(lane < SSD_HEADS, cs, total - cs + a)
    dtb_ref[...] = dt.astype(BF16)
    dob_ref[...] = jnp.exp(cs).astype(BF16)
    dsb_ref[...] = jnp.exp(total - cs).astype(BF16)
    cst_ref[...] = jnp.transpose(cs)
    dc_ref[...] = jnp.broadcast_to(jnp.exp(total), dc_ref.shape)


def _dt_prep(u_small, dt_bias, a_coef):
    b, n, _ = u_small.shape
    q = SSD_CHUNK
    nc = n // q
    vec = lambda: pl.BlockSpec((1, 128), lambda bi, c: (0, 0))
    tok = lambda: pl.BlockSpec((None, q, 128), lambda bi, c: (bi, c, 0))
    return pl.pallas_call(
        _dt_kernel,
        grid=(b, nc),
        in_specs=[tok(), vec(), vec()],
        out_specs=[tok(), tok(), tok(),
                   pl.BlockSpec((None, None, 128, q), lambda bi, c: (bi, c, 0, 0)),
                   pl.BlockSpec((None, None, 8, 128), lambda bi, c: (bi, c, 0, 0))],
        out_shape=[jax.ShapeDtypeStruct((b, n, 128), BF16)] * 3
        + [jax.ShapeDtypeStruct((b, nc, 128, q), F32), jax.ShapeDtypeStruct((b, nc, 8, 128), F32)],
        compiler_params=_cparams(("parallel", "parallel")),
        name="dt_prep",
    )(u_small, dt_bias, a_coef)


_TN = (((0,), (0,)), ((), ()))


def _ssd_kernel(x_ref, b_ref, c_ref, dt_ref, do_ref, ds_ref, cst_ref, dc_ref, e_ref, init_ref,
                y_ref, fin_ref, st_ref, *, direction, cpb):
    q = SSD_CHUNK
    step = pl.program_id(1)

    @pl.when(step == 0)
    def _():
        st_ref[...] = init_ref[...]

    ri = lax.broadcasted_iota(jnp.int32, (q, q), 0)
    ci = lax.broadcasted_iota(jnp.int32, (q, q), 1)
    mask = (ri >= ci) if direction == 0 else (ri <= ci)
    lane = lax.broadcasted_iota(jnp.int32, (q, 2 * SSD_HEADDIM), 1)
    left = lane < SSD_HEADDIM

    def chunk_body(cc, carry):
        c = cc if direction == 0 else cpb - 1 - cc
        rows = pl.ds(pl.multiple_of(c * q, q), q)
        dtb = dt_ref[rows, :]
        dob = do_ref[rows, :]
        dsb = ds_ref[rows, :]

        def group_body(g, carry2):
            eg = e_ref[g]
            dtx = jnp.dot(dtb, eg, preferred_element_type=F32)
            dox = jnp.dot(dob, eg, preferred_element_type=F32)
            dsx = jnp.dot(dsb, eg, preferred_element_type=F32)
            xdt = x_ref[g, rows, :].astype(F32) * dtx
            xdt_b = xdt.astype(BF16)
            xds_b = (xdt * dsx).astype(BF16)
            bg = b_ref[g, rows, :]
            cg = c_ref[g, rows, :]
            cb = lax.dot_general(cg, bg, _NT, preferred_element_type=F32)
            st = st_ref[g]
            y_off = jnp.dot(cg, st.astype(BF16), preferred_element_type=F32) * dox
            g8 = cst_ref[c, pl.ds(pl.multiple_of(direction * SSD_HEADS + g * HEADS_PER_GROUP, 8), 8), :]
            g8t = jnp.transpose(g8)
            pieces = []
            for pr in range(HEADS_PER_GROUP // 2):
                pair = xdt_b[:, pr * 128:(pr + 1) * 128]
                acc = None
                for half in range(2):
                    r = 2 * pr + half
                    seg = g8t[:, r:r + 1] - g8[r:r + 1, :]
                    m = (cb * jnp.exp(jnp.where(mask, seg, -jnp.inf))).astype(BF16)
                    xin = jnp.where(left if half == 0 else jnp.logical_not(left), pair, jnp.zeros_like(pair))
                    d = jnp.dot(m, xin, preferred_element_type=F32)
                    acc = d if acc is None else acc + d
                pieces.append(acc)
            y = jnp.concatenate(pieces, axis=1) + y_off
            y_ref[g, rows, :] = y.astype(BF16)
            st_new = lax.dot_general(bg, xds_b, _TN, preferred_element_type=F32)
            st_ref[g] = st * dc_ref[c, pl.ds(g, 1), :] + st_new
            return carry2

        lax.fori_loop(0, SSD_GROUPS, group_body, 0)
        return carry

    lax.fori_loop(0, cpb, chunk_body, 0)

    @pl.when(step == pl.num_programs(1) - 1)
    def _():
        fin_ref[...] = st_ref[...]


def _ssd(xg, bg, cg, dtb, dob, dsb, cst, dcx, e_mat, init, direction, cpb):
    b, g, n, _ = xg.shape
    q = SSD_CHUNK
    t = q * cpb
    nt = n // t
    ti = (lambda i: i) if direction == 0 else (lambda i: nt - 1 - i)
    grp = lambda w: pl.BlockSpec((None, g, t, w), lambda bi, i: (bi, 0, ti(i), 0))
    tok = lambda: pl.BlockSpec((None, t, 128), lambda bi, i: (bi, ti(i), 0))
    return pl.pallas_call(
        functools.partial(_ssd_kernel, direction=direction, cpb=cpb),
        grid=(b, nt),
        in_specs=[grp(GROUP_WIDTH), grp(SSD_STATE), grp(SSD_STATE), tok(), tok(), tok(),
                  pl.BlockSpec((None, cpb, 128, q), lambda bi, i: (bi, ti(i), 0, 0)),
                  pl.BlockSpec((None, cpb, None, g, GROUP_WIDTH), lambda bi, i: (bi, ti(i), direction, 0, 0)),
                  pl.BlockSpec((None, g, 128, GROUP_WIDTH), lambda bi, i: (direction, 0, 0, 0)),
                  pl.BlockSpec((None, g, SSD_STATE, GROUP_WIDTH), lambda bi, i: (bi, 0, 0, 0))],
        out_specs=[grp(GROUP_WIDTH),
                   pl.BlockSpec((None, g, SSD_STATE, GROUP_WIDTH), lambda bi, i: (bi, 0, 0, 0))],
        out_shape=[jax.ShapeDtypeStruct((b, g, n, GROUP_WIDTH), BF16),
                   jax.ShapeDtypeStruct((b, g, SSD_STATE, GROUP_WIDTH), F32)],
        scratch_shapes=[pltpu.VMEM((g, SSD_STATE, GROUP_WIDTH), F32)],
        compiler_params=_cparams(("parallel", "arbitrary")),
        name="ssd_fwd" if direction == 0 else "ssd_bwd",
    )(xg, bg, cg, dtb, dob, dsb, cst, dcx, e_mat, init)


def _ssd_out_kernel(x_ref, yf_ref, yb_ref, z_ref, d_ref, g_ref, w_ref, o_ref):
    ys = []
    ss = None
    for g in range(SSD_GROUPS):
        cols = slice(g * GROUP_WIDTH, (g + 1) * GROUP_WIDTH)
        y = d_ref[:, cols] * x_ref[g].astype(F32) + yf_ref[g].astype(F32) + yb_ref[g].astype(F32)
        y = y * _silu(z_ref[:, cols].astype(F32))
        sq = jnp.sum(y * y, axis=-1, keepdims=True)
        ss = sq if ss is None else ss + sq
        ys.append(y)
    inv = lax.rsqrt(ss * (1.0 / SSD_INNER) + NORM_EPS)
    acc = None
    for g in range(SSD_GROUPS):
        cols = slice(g * GROUP_WIDTH, (g + 1) * GROUP_WIDTH)
        yn = (ys[g] * inv * g_ref[:, cols]).astype(BF16)
        d = jnp.dot(yn, w_ref[cols, :], preferred_element_type=F32)
        acc = d if acc is None else acc + d
    o_ref[...] = acc.astype(BF16)


def _ssd_out(xg, yf, yb, u_main, d_exp, g_ssd, w, tm):
    b, g, n, gw = xg.shape
    dm = w.shape[1]
    grp = lambda: pl.BlockSpec((None, g, tm, gw), lambda bi, i: (bi, 0, i, 0))
    return pl.pallas_call(
        _ssd_out_kernel,
        grid=(b, n // tm),
        in_specs=[grp(), grp(), grp(),
                  pl.BlockSpec((None, tm, SSD_INNER), lambda bi, i: (bi, i, U_Z // SSD_INNER)),
                  pl.BlockSpec((1, SSD_INNER), lambda bi, i: (0, 0)),
                  pl.BlockSpec((1, SSD_INNER), lambda bi, i: (0, 0)),
                  pl.BlockSpec((SSD_INNER, dm), lambda bi, i: (0, 0))],
        out_specs=pl.BlockSpec((None, tm, dm), lambda bi, i: (bi, i, 0)),
        out_shape=jax.ShapeDtypeStruct((b, n, dm), BF16),
        compiler_params=_cparams(("parallel", "parallel")),
        name="ssd_out",
    )(xg, yf, yb, u_main, d_exp, g_ssd, w)


def _pack_pairs(x):
    k = x.shape[1] // 2
    u = lax.bitcast_convert_type(x.astype(BF16).astype(F32), U32)
    return (u[:, k:] & jnp.uint32(0xFFFF0000)) | (u[:, :k] >> 16)


def _unpack_pairs(u):
    lo = lax.bitcast_convert_type(u << 16, F32)
    hi = lax.bitcast_convert_type(u & jnp.uint32(0xFFFF0000), F32)
    return jnp.concatenate([lo, hi], axis=1)


def _merge_kernel(s_ref, o_ref, gate_ref, x_ref, ga_ref, shf_ref, scf_ref, gpost_ref, gpre_ref,
                  wo_ref, wout_ref, wrh_ref, wrl_ref, br_ref, x1_ref, hp_ref, lg_ref):
    d = x_ref.shape[1]
    y_mla = jnp.dot(o_ref[...], wo_ref[...], preferred_element_type=F32)
    gates = gate_ref[...].astype(F32)
    mix = jax.nn.sigmoid(gates[:, :d]) * s_ref[...].astype(F32) + jax.nn.sigmoid(gates[:, d:]) * y_mla
    ym = jnp.dot(mix.astype(BF16), wout_ref[...], preferred_element_type=F32)
    x1 = x_ref[...] + ga_ref[...] * (_rms(ym) * gpost_ref[...])
    x1_ref[...] = x1
    h2 = _rms(x1) * (gpre_ref[...] * (1.0 + scf_ref[...])) + shf_ref[...]
    hp_ref[...] = _pack_pairs(h2)
    hh = h2.astype(BF16)
    hl = (h2 - hh.astype(F32)).astype(BF16)
    lg = (jnp.dot(hh, wrh_ref[...], preferred_element_type=F32)
          + jnp.dot(hh, wrl_ref[...], preferred_element_type=F32)
          + jnp.dot(hl, wrh_ref[...], preferred_element_type=F32))
    lg_ref[...] = lg + br_ref[...]


def _merge(s, o, u_main, x, g_a, sh_f, sc_f, g_post, g_pre, w_o, w_out, wr_hi, wr_lo, b_r, tm):
    b, n, d = x.shape
    tokd = lambda: pl.BlockSpec((None, tm, d), lambda bi, i: (bi, i, 0))
    modv = lambda: pl.BlockSpec((None, 1, d), lambda bi, i: (bi, 0, 0))
    vec = lambda w: pl.BlockSpec((1, w), lambda bi, i: (0, 0))
    mat = lambda r, c: pl.BlockSpec((r, c), lambda bi, i: (0, 0))
    return pl.pallas_call(
        _merge_kernel,
        grid=(b, n // tm),
        in_specs=[tokd(), tokd(),
                  pl.BlockSpec((None, tm, 2 * d), lambda bi, i: (bi, i, U_GATE // (2 * d))),
                  tokd(), modv(), modv(), modv(), vec(d), vec(d),
                  mat(d, d), mat(d, d), mat(d, 128), mat(d, 128), vec(128)],
        out_specs=[tokd(),
                   pl.BlockSpec((None, tm, d // 2), lambda bi, i: (bi, i, 0)),
                   pl.BlockSpec((None, tm, 128), lambda bi, i: (bi, i, 0))],
        out_shape=[jax.ShapeDtypeStruct((b, n, d), F32),
                   jax.ShapeDtypeStruct((b, n, d // 2), U32),
                   jax.ShapeDtypeStruct((b, n, 128), F32)],
        compiler_params=_cparams(("parallel", "parallel")),
        name="merge",
    )(s, o, u_main, x, g_a, sh_f, sc_f, g_post, g_pre, w_o, w_out, wr_hi, wr_lo, b_r)


def _gather_kernel(idx_ref, src_ref, dst_ref, sem, *, rows):
    base = pl.program_id(0) * rows

    def issue(r, carry):
        pltpu.make_async_copy(src_ref.at[pl.ds(idx_ref[base + r], 1), :],
                              dst_ref.at[pl.ds(base + r, 1), :], sem).start()
        return carry

    lax.fori_loop(0, rows, issue, 0)

    def drain(r, carry):
        pltpu.make_async_copy(src_ref.at[pl.ds(0, 1), :], dst_ref.at[pl.ds(base, 1), :], sem).wait()
        return carry

    lax.fori_loop(0, rows, drain, 0)


def _gather_rows(src, idx, rows_per_step=512):
    m = idx.shape[0]
    return pl.pallas_call(
        functools.partial(_gather_kernel, rows=rows_per_step),
        grid_spec=pltpu.PrefetchScalarGridSpec(
            num_scalar_prefetch=1,
            grid=(m // rows_per_step,),
            in_specs=[pl.BlockSpec(memory_space=pl.ANY)],
            out_specs=pl.BlockSpec(memory_space=pl.ANY),
            scratch_shapes=[pltpu.SemaphoreType.DMA(())]),
        out_shape=jax.ShapeDtypeStruct((m, src.shape[1]), src.dtype),
        compiler_params=pltpu.CompilerParams(dimension_semantics=("arbitrary",)),
        name="gather_rows",
    )(idx, src)


def _expert_kernel(be_ref, nu_ref, x_ref, wg_ref, wu_ref, wd_ref, y_ref):
    i = pl.program_id(0)

    @pl.when(i < nu_ref[0])
    def _():
        x = _unpack_pairs(x_ref[...]).astype(BF16)
        hg = jnp.dot(x, wg_ref[...], preferred_element_type=F32)
        hu = jnp.dot(x, wu_ref[...], preferred_element_type=F32)
        a = (_silu(hg) * hu).astype(BF16)
        y_ref[...] = _pack_pairs(jnp.dot(a, wd_ref[...], preferred_element_type=F32))

    @pl.when(i >= nu_ref[0])
    def _():
        y_ref[...] = jnp.zeros(y_ref.shape, y_ref.dtype)


def _experts(x_buf, blk_e, n_used, w_gate, w_up, w_down):
    rows, dh = x_buf.shape
    ne, d, ff = w_gate.shape
    nb = rows // MOE_BLOCK
    return pl.pallas_call(
        _expert_kernel,
        grid_spec=pltpu.PrefetchScalarGridSpec(
            num_scalar_prefetch=2,
            grid=(nb,),
            in_specs=[pl.BlockSpec((MOE_BLOCK, dh), lambda i, be, nu: (i, 0)),
                      pl.BlockSpec((None, d, ff), lambda i, be, nu: (be[i], 0, 0)),
                      pl.BlockSpec((None, d, ff), lambda i, be, nu: (be[i], 0, 0)),
                      pl.BlockSpec((None, ff, d), lambda i, be, nu: (be[i], 0, 0))],
            out_specs=pl.BlockSpec((MOE_BLOCK, dh), lambda i, be, nu: (i, 0))),
        out_shape=jax.ShapeDtypeStruct((rows, dh), U32),
        compiler_params=_cparams(("arbitrary",)),
        name="experts",
    )(blk_e, n_used, x_buf, w_gate, w_up, w_down)


def _final_kernel(y0_ref, y1_ref, w_ref, x1_ref, gf_ref, gpost_ref, o_ref):
    w = w_ref[...]
    f = w[:, 0:1] * _unpack_pairs(y0_ref[...]) + w[:, 1:2] * _unpack_pairs(y1_ref[...])
    o_ref[...] = x1_ref[...] + gf_ref[...] * (_rms(f) * gpost_ref[...])


def _final(yg, wts, x1, g_f, g_post, tm):
    b, n, d = x1.shape
    return pl.pallas_call(
        _final_kernel,
        grid=(b, n // tm),
        in_specs=[pl.BlockSpec((None, None, tm, d // 2), lambda bi, i: (0, bi, i, 0)),
                  pl.BlockSpec((None, None, tm, d // 2), lambda bi, i: (1, bi, i, 0)),
                  pl.BlockSpec((None, tm, 128), lambda bi, i: (bi, i, 0)),
                  pl.BlockSpec((None, tm, d), lambda bi, i: (bi, i, 0)),
                  pl.BlockSpec((None, 1, d), lambda bi, i: (bi, 0, 0)),
                  pl.BlockSpec((1, d), lambda bi, i: (0, 0))],
        out_specs=pl.BlockSpec((None, tm, d), lambda bi, i: (bi, i, 0)),
        out_shape=jax.ShapeDtypeStruct((b, n, d), F32),
        compiler_params=_cparams(("parallel", "parallel")),
        name="final",
    )(yg, yg, wts, x1, g_f, g_post)


def _route(logits):
    t = logits.shape[0]
    g_prob = jax.nn.softmax(logits[:, :N_GROUPS_E], axis=-1)
    g_sel = jnp.argmax(g_prob, axis=-1)
    g_top = jnp.max(g_prob, axis=-1)
    e_logits = logits[:, N_GROUPS_E:N_GROUPS_E + N_EXPERTS].reshape(t, N_GROUPS_E, EXPERTS_PER_GROUP)
    e_logits = jnp.take_along_axis(e_logits, g_sel[:, None, None], axis=1)[:, 0]
    e_top, e_sel = lax.top_k(jax.nn.softmax(e_logits, axis=-1), TOP_K)
    weights = g_top[:, None] * e_top / jnp.sum(e_top, axis=-1, keepdims=True)
    expert_id = (g_sel[:, None] * EXPERTS_PER_GROUP + e_sel).astype(jnp.int32)
    return expert_id, weights


def _dispatch_plan(expert_id):
    t = expert_id.shape[0]
    n_assign = t * TOP_K
    eid = expert_id.reshape(n_assign)
    onehot = (eid[:, None] == jnp.arange(N_EXPERTS, dtype=jnp.int32)[None, :]).astype(jnp.int32)
    csum = jnp.cumsum(onehot, axis=0)
    counts = csum[-1]
    rank = jnp.take_along_axis(csum, eid[:, None], axis=1)[:, 0] - 1
    start = jnp.cumsum(counts) - counts
    nblk_e = (counts + MOE_BLOCK - 1) // MOE_BLOCK
    bend = jnp.cumsum(nblk_e)
    pstart = (bend - nblk_e) * MOE_BLOCK
    pos = (pstart[eid] + rank).astype(jnp.int32)
    n_blocks = n_assign // MOE_BLOCK + N_EXPERTS
    n_used = bend[-1].astype(jnp.int32)
    blk = jnp.arange(n_blocks, dtype=jnp.int32)
    blk_e = jnp.searchsorted(bend, jnp.minimum(blk, n_used - 1), side='right').astype(jnp.int32)
    blk_e = jnp.clip(blk_e, 0, N_EXPERTS - 1)
    order = jnp.argsort(eid, stable=True).astype(jnp.int32)
    rowi = jnp.arange(n_blocks * MOE_BLOCK, dtype=jnp.int32)
    re = blk_e[rowi // MOE_BLOCK]
    off = rowi - pstart[re]
    valid = (off < counts[re]) & (rowi // MOE_BLOCK < n_used)
    src = jnp.clip(start[re] + off, 0, n_assign - 1)
    tid_buf = jnp.where(valid, order[src] // TOP_K, 0).astype(jnp.int32)
    return pos, tid_buf, blk_e, n_used.reshape(1)


def _rope_tables(n):
    rows = n // GRID_W
    row = jnp.repeat(jnp.arange(rows), GRID_W).astype(F32)
    col = jnp.tile(jnp.arange(GRID_W), rows).astype(F32)
    half = QK_ROPE // 2
    freqs = ROPE_THETA ** (-jnp.arange(0, half, 2, dtype=F32) / half)
    ar = row[:, None] * freqs
    ac = col[:, None] * freqs
    cos = jnp.concatenate([jnp.cos(ar), jnp.cos(ar), jnp.cos(ac), jnp.cos(ac)], axis=1)
    sin = jnp.concatenate([-jnp.sin(ar), jnp.sin(ar), -jnp.sin(ac), jnp.sin(ac)], axis=1)
    return cos.T, sin.T, cos, sin


def _pick(n, pref):
    return pref if n % pref == 0 else n


def kernel(x, c, ctx, c_ctx, w_ada, b_ada, g_pre_mix, g_post_mix, g_pre_ffn, g_post_ffn, w_in, g_q, w_uq,
           g_kv, w_ukv, w_o_mla, conv_w, conv_b, a_log, dt_bias, d_skip, g_ssd, w_ssd_out, w_out,
           w_router_group, b_router_group, w_router_expert, b_router_expert, w_exp_gate, w_exp_up, w_exp_down):
    assert w_ada.shape[0] == 1, "single-layer block"
    b, n, d = x.shape
    n_ctx = ctx.shape[1]
    h = MLA_HEADS

    w_in0 = w_in[0]
    o_ckv = Q_LORA
    o_kr = o_ckv + KV_LORA
    o_z = o_kr + QK_ROPE
    o_xbc = o_z + SSD_INNER
    o_dt = o_xbc + SSD_INNER + 2 * SSD_GROUPS * SSD_STATE
    o_gate = o_dt + 2 * SSD_HEADS
    w_kr = w_in0[:, o_kr:o_z]
    swap = jnp.concatenate([jnp.arange(16, 32), jnp.arange(0, 16), jnp.arange(48, 64), jnp.arange(32, 48)])
    w_main = jnp.concatenate([w_in0[:, o_z:o_xbc], w_in0[:, o_gate:], w_in0[:, o_xbc:o_dt],
                              w_in0[:, :o_ckv], w_in0[:, o_ckv:o_kr]], axis=1).astype(BF16)
    w_small = jnp.concatenate([w_in0[:, o_dt:o_gate], w_kr, w_kr[:, swap]], axis=1).astype(BF16)
    wq_t = w_uq[0].reshape(Q_LORA, h, QK_DIM).transpose(1, 2, 0).astype(BF16)
    wkv = w_ukv[0].reshape(KV_LORA, h, QK_NOPE + V_HEAD)
    wk = wkv[:, :, :QK_NOPE].transpose(1, 0, 2).astype(BF16)
    wv_t = wkv[:, :, QK_NOPE:].transpose(1, 2, 0).astype(BF16)
    w_o = w_o_mla[0].astype(BF16)
    w_so = w_ssd_out[0].astype(BF16)
    w_mix = w_out[0].astype(BF16)
    w_r = jnp.concatenate([w_router_group[0], w_router_expert[0],
                           jnp.zeros((d, 128 - N_GROUPS_E - N_EXPERTS), F32)], axis=1)
    wr_hi = w_r.astype(BF16)
    wr_lo = (w_r - wr_hi.astype(F32)).astype(BF16)
    b_r = jnp.concatenate([b_router_group[0], b_router_expert[0],
                           jnp.zeros((128 - N_GROUPS_E - N_EXPERTS,), F32)]).reshape(1, 128)
    wg = w_exp_gate[0].astype(BF16)
    wu = w_exp_up[0].astype(BF16)
    wd = w_exp_down[0].astype(BF16)
    vec = lambda v: v.reshape(1, -1)

    c8 = jnp.concatenate([c, c_ctx[None, :], jnp.zeros((8 - b - 1, d), F32)], axis=0)
    mod = _ada(c8, w_ada[0], b_ada[0])
    sh_a, sc_a, g_a, sh_f, sc_f, g_f = [mod[:, i * d:(i + 1) * d] for i in range(N_MOD)]
    lat = lambda m: m[:b].reshape(b, 1, d)
    cmod = lambda m: jnp.broadcast_to(m[b:b + 1].reshape(1, 1, d), (b, 1, d))

    u_main, u_small = _inproj(x, lat(sc_a), lat(sh_a), vec(g_pre_mix[0]), w_main, w_small, _pick(n, 512))
    uc_main, uc_small = _inproj(ctx, cmod(sc_a), cmod(sh_a), vec(g_pre_mix[0]), w_main, w_small, n_ctx)

    tn = _pick(n, 512)
    q_t, k_l, v_l = _qkv(u_main, u_small, vec(g_q[0]), vec(g_kv[0]), wq_t, wk, wv_t, _rope_tables(n), tn)
    k_c, v_c = _qkv(uc_main, uc_small, None, vec(g_kv[0]), None, wk, wv_t, None, n_ctx)
    o_mla = _attention(q_t, k_l, v_l, k_c, v_c, _pick(tn, 256))

    a_coef = -jnp.exp(a_log[0].astype(F32)).reshape(1, 2 * SSD_HEADS)
    dtb_flat = dt_bias[0].astype(F32).reshape(1, 2 * SSD_HEADS)
    hh = jnp.arange(2 * SSD_HEADS)
    lane_head = jnp.arange(GROUP_WIDTH) // SSD_HEADDIM
    e_mat = (hh[None, None, :, None] == (jnp.arange(2)[:, None, None, None] * SSD_HEADS
                                        + jnp.arange(SSD_GROUPS)[None, :, None, None] * HEADS_PER_GROUP
                                        + lane_head[None, None, None, :])).astype(BF16)

    def ssd_inputs(um, us, tc):
        xg, bg, cg = _conv(um, conv_w[0], conv_b[0], tc)
        dtb, dob, dsb, cst, dc = _dt_prep(us, dtb_flat, a_coef)
        nc = dc.shape[1]
        dcx = jnp.repeat(dc[:, :, 0, :].reshape(b, nc, 2, SSD_GROUPS, HEADS_PER_GROUP), SSD_HEADDIM, axis=-1)
        return xg, bg, cg, dtb, dob, dsb, cst, dcx

    zero_state = jnp.zeros((b, SSD_GROUPS, SSD_STATE, GROUP_WIDTH), F32)
    ctx_in = ssd_inputs(uc_main, uc_small, n_ctx)
    lat_in = ssd_inputs(u_main, u_small, _pick(n, 256))
    cpb_c = n_ctx // SSD_CHUNK
    cpb = _pick(n, 512) // SSD_CHUNK
    ys = []
    for direction in range(2):
        _, s0 = _ssd(*ctx_in, e_mat, zero_state, direction, cpb_c)
        y_dir, _ = _ssd(*lat_in, e_mat, s0, direction, cpb)
        ys.append(y_dir)
    d_exp = jnp.repeat(d_skip[0].astype(F32), SSD_HEADDIM).reshape(1, SSD_INNER)
    s_l = _ssd_out(lat_in[0], ys[0], ys[1], u_main, d_exp, vec(g_ssd[0]), w_so, _pick(n, 256))

    x1, h_pack, logits = _merge(s_l, o_mla, u_main, x, lat(g_a), lat(sh_f), lat(sc_f), vec(g_post_mix[0]),
                                vec(g_pre_ffn[0]), w_o, w_mix, wr_hi, wr_lo, b_r, _pick(n, 256))

    t = b * n
    expert_id, weights = _route(logits.reshape(t, 128))
    pos, tid_buf, blk_e, n_used = _dispatch_plan(expert_id)
    x_buf = _gather_rows(h_pack.reshape(t, d // 2), tid_buf)
    y_buf = _experts(x_buf, blk_e, n_used, wg, wu, wd)
    yg = _gather_rows(y_buf, pos.reshape(t, TOP_K).T.reshape(TOP_K * t))
    wts = jnp.concatenate([weights, jnp.zeros((t, 128 - TOP_K), F32)], axis=1).reshape(b, n, 128)
    return _final(yg.reshape(TOP_K, b, n, d // 2), wts, x1, lat(g_f), vec(g_post_ffn[0]), _pick(n, 512))
```

```python
import functools
import math

import jax
import jax.numpy as jnp
from jax import lax
from jax.experimental import pallas as pl
from jax.experimental.pallas import tpu as pltpu

F32 = jnp.float32
BF16 = jnp.bfloat16
U32 = jnp.uint32

GRID_W = 64
N_MOD = 6
NORM_EPS = 1e-6
MLA_HEADS = 16
Q_LORA = 512
KV_LORA = 512
QK_NOPE = 128
QK_ROPE = 64
QK_DIM = QK_NOPE + QK_ROPE
V_HEAD = 128
ROPE_THETA = 10000.0
ATTN_SCALE = QK_DIM ** -0.5
SSD_HEADDIM = 64
SSD_HEADS = 64
SSD_INNER = SSD_HEADS * SSD_HEADDIM
SSD_GROUPS = 8
SSD_STATE = 128
SSD_CONV = 5
HEADS_PER_GROUP = SSD_HEADS // SSD_GROUPS
GROUP_WIDTH = HEADS_PER_GROUP * SSD_HEADDIM
N_GROUPS_E = 8
EXPERTS_PER_GROUP = 8
N_EXPERTS = N_GROUPS_E * EXPERTS_PER_GROUP
TOP_K = 2
LOG2E = 1.4426950408889634

U_Z = 0
U_GATE = U_Z + SSD_INNER
U_XBC = U_GATE + 2 * 2048
U_CQ = U_XBC + SSD_INNER + 2 * SSD_GROUPS * SSD_STATE
U_CKV = U_CQ + Q_LORA
U_MAIN = U_CKV + KV_LORA
U_SMALL = 256

SSD_CHUNK = 128
MOE_BLOCK = 256
VMEM_LIMIT = 56 * 1024 * 1024


def _cparams(sem):
    return pltpu.CompilerParams(dimension_semantics=sem, vmem_limit_bytes=VMEM_LIMIT)


def _silu(x):
    return x * jax.nn.sigmoid(x)


def _rms(x, eps=NORM_EPS):
    return x * lax.rsqrt(jnp.mean(x * x, axis=-1, keepdims=True) + eps)


def _ada_kernel(c_ref, w_ref, b_ref, o_ref):
    c = c_ref[...]
    o_ref[...] = jnp.dot(_silu(c).astype(BF16), w_ref[...].astype(BF16),
                         preferred_element_type=F32) + b_ref[...]


def _ada(c8, w_ada, b_ada):
    d, n = w_ada.shape
    tn = 1536
    return pl.pallas_call(
        _ada_kernel,
        grid=(n // tn,),
        in_specs=[pl.BlockSpec((8, d), lambda j: (0, 0)),
                  pl.BlockSpec((d, tn), lambda j: (0, j)),
                  pl.BlockSpec((1, tn), lambda j: (0, j))],
        out_specs=pl.BlockSpec((8, tn), lambda j: (0, j)),
        out_shape=jax.ShapeDtypeStruct((8, n), F32),
        compiler_params=_cparams(("parallel",)),
        name="ada",
    )(c8, w_ada, b_ada.reshape(1, n))


def _inproj_kernel(x_ref, sc_ref, sh_ref, g_ref, wm_ref, ws_ref, om_ref, os_ref, h_scr):
    @pl.when(pl.program_id(2) == 0)
    def _():
        y = _rms(x_ref[...]) * (g_ref[...] * (1.0 + sc_ref[...])) + sh_ref[...]
        hb = y.astype(BF16)
        h_scr[...] = hb
        os_ref[...] = jnp.dot(hb, ws_ref[...], preferred_element_type=F32)

    om_ref[...] = jnp.dot(h_scr[...], wm_ref[...], preferred_element_type=F32).astype(BF16)


def _inproj(x, scale, shift, g, w_main, w_small, tm):
    b, n, d = x.shape
    tn = 1536
    return pl.pallas_call(
        _inproj_kernel,
        grid=(b, n // tm, U_MAIN // tn),
        in_specs=[pl.BlockSpec((None, tm, d), lambda bi, i, j: (bi, i, 0)),
                  pl.BlockSpec((None, 1, d), lambda bi, i, j: (bi, 0, 0)),
                  pl.BlockSpec((None, 1, d), lambda bi, i, j: (bi, 0, 0)),
                  pl.BlockSpec((1, d), lambda bi, i, j: (0, 0)),
                  pl.BlockSpec((d, tn), lambda bi, i, j: (0, j)),
                  pl.BlockSpec((d, U_SMALL), lambda bi, i, j: (0, 0))],
        out_specs=[pl.BlockSpec((None, tm, tn), lambda bi, i, j: (bi, i, j)),
                   pl.BlockSpec((None, tm, U_SMALL), lambda bi, i, j: (bi, i, 0))],
        out_shape=[jax.ShapeDtypeStruct((b, n, U_MAIN), BF16),
                   jax.ShapeDtypeStruct((b, n, U_SMALL), F32)],
        scratch_shapes=[pltpu.VMEM((tm, d), BF16)],
        compiler_params=_cparams(("parallel", "parallel", "arbitrary")),
        name="in_proj",
    )(x, scale, shift, g, w_main, w_small)


_NT = (((1,), (1,)), ((), ()))


def _qkv_kernel(*refs, rope, with_q):
    if rope:
        (cq_ref, ckv_ref, kr_ref, gq_ref, gkv_ref, wq_ref, wk_ref, wv_ref,
         cost_ref, sint_ref, cos_ref, sin_ref, q_ref, k_ref, v_ref) = refs
    else:
        ckv_ref, kr_ref, gkv_ref, wk_ref, wv_ref, k_ref, v_ref = refs
    ckvn = (_rms(ckv_ref[...].astype(F32)) * gkv_ref[...]).astype(BF16)
    kr = kr_ref[...]
    kpe = kr[:, :QK_ROPE]
    if rope:
        kpe = kpe * cos_ref[...] + kr[:, QK_ROPE:] * sin_ref[...]
    kpe = kpe.astype(BF16)
    if with_q:
        cqn = (_rms(cq_ref[...].astype(F32)) * gq_ref[...]).astype(BF16)
        cost = cost_ref[...]
        sint = sint_ref[...]

    def head(h, carry):
        if with_q:
            qt = lax.dot_general(wq_ref[h], cqn, _NT, preferred_element_type=F32)
            qp = qt[QK_NOPE:]
            qsw = jnp.concatenate([qp[16:32], qp[0:16], qp[48:64], qp[32:48]], axis=0)
            qp = qp * cost + qsw * sint
            q_ref[h, :QK_NOPE, :] = (qt[:QK_NOPE] * (ATTN_SCALE * LOG2E)).astype(BF16)
            q_ref[h, QK_NOPE:, :] = (qp * (ATTN_SCALE * LOG2E)).astype(BF16)
        kn = jnp.dot(ckvn, wk_ref[h], preferred_element_type=F32)
        k_ref[h, :, :QK_NOPE] = kn.astype(BF16)
        k_ref[h, :, QK_NOPE:] = kpe
        vt = lax.dot_general(wv_ref[h], ckvn, _NT, preferred_element_type=F32)
        v_ref[h] = vt.astype(BF16)
        return carry

    lax.fori_loop(0, MLA_HEADS, head, 0)


def _qkv(u_main, u_small, g_q, g_kv, wq_t, wk, wv_t, tables, tn):
    b, n, _ = u_main.shape
    nt = n // tn
    h = MLA_HEADS
    rope = tables is not None
    in_specs = []
    args = []
    if rope:
        in_specs.append(pl.BlockSpec((None, tn, Q_LORA), lambda bi, i: (bi, i, U_CQ // Q_LORA)))
        args.append(u_main)
    in_specs += [pl.BlockSpec((None, tn, KV_LORA), lambda bi, i: (bi, i, U_CKV // KV_LORA)),
                 pl.BlockSpec((None, tn, 128), lambda bi, i: (bi, i, 1))]
    args += [u_main, u_small]
    if rope:
        in_specs.append(pl.BlockSpec((1, Q_LORA), lambda bi, i: (0, 0)))
        args.append(g_q)
    in_specs.append(pl.BlockSpec((1, KV_LORA), lambda bi, i: (0, 0)))
    args.append(g_kv)
    if rope:
        in_specs.append(pl.BlockSpec((h, QK_DIM, Q_LORA), lambda bi, i: (0, 0, 0)))
        args.append(wq_t)
    in_specs += [pl.BlockSpec((h, KV_LORA, QK_NOPE), lambda bi, i: (0, 0, 0)),
                 pl.BlockSpec((h, V_HEAD, KV_LORA), lambda bi, i: (0, 0, 0))]
    args += [wk, wv_t]
    out_specs = []
    out_shape = []
    if rope:
        cos_t, sin_t, cos, sin = tables
        in_specs += [pl.BlockSpec((QK_ROPE, tn), lambda bi, i: (0, i)),
                     pl.BlockSpec((QK_ROPE, tn), lambda bi, i: (0, i)),
                     pl.BlockSpec((tn, QK_ROPE), lambda bi, i: (i, 0)),
                     pl.BlockSpec((tn, QK_ROPE), lambda bi, i: (i, 0))]
        args += [cos_t, sin_t, cos, sin]
        out_specs.append(pl.BlockSpec((None, None, h, QK_DIM, tn), lambda bi, i: (bi, i, 0, 0, 0)))
        out_shape.append(jax.ShapeDtypeStruct((b, nt, h, QK_DIM, tn), BF16))
    out_specs += [pl.BlockSpec((None, None, h, tn, QK_DIM), lambda bi, i: (bi, i, 0, 0, 0)),
                  pl.BlockSpec((None, None, h, V_HEAD, tn), lambda bi, i: (bi, i, 0, 0, 0))]
    out_shape += [jax.ShapeDtypeStruct((b, nt, h, tn, QK_DIM), BF16),
                  jax.ShapeDtypeStruct((b, nt, h, V_HEAD, tn), BF16)]
    return pl.pallas_call(
        functools.partial(_qkv_kernel, rope=rope, with_q=rope),
        grid=(b, nt),
        in_specs=in_specs,
        out_specs=out_specs,
        out_shape=out_shape,
        compiler_params=_cparams(("parallel", "parallel")),
        name="qkv_rope" if rope else "kv_ctx",
    )(*args)


def _attn_kernel(q_ref, kl_ref, kc_ref, vl_ref, vc_ref, o_ref, sa_ref, sb_ref):
    qt = q_ref[...]
    tq = qt.shape[1]
    nt = kl_ref.shape[0]

    def scores(kb):
        return jnp.dot(kb, qt, preferred_element_type=F32)

    def softmax_pv(s, vb, carry):
        m, l, acc = carry
        m_new = jnp.maximum(m, jnp.max(s, axis=0, keepdims=True))
        alpha = jnp.exp2(m - m_new)
        p = jnp.exp2(s - m_new)
        l = alpha * l + jnp.sum(p, axis=0, keepdims=True)
        acc = alpha * acc + jnp.dot(vb, p.astype(BF16), preferred_element_type=F32)
        return m_new, l, acc

    init = (jnp.full((1, tq), -jnp.inf, F32), jnp.zeros((1, tq), F32), jnp.zeros((V_HEAD, tq), F32))
    sa_ref[...] = scores(kl_ref[0])
    carry = softmax_pv(scores(kc_ref[0]), vc_ref[0], init)

    def pair(j, c):
        sb_ref[...] = scores(kl_ref[2 * j + 1])
        c = softmax_pv(sa_ref[...], vl_ref[2 * j], c)
        sa_ref[...] = scores(kl_ref[jnp.minimum(2 * j + 2, nt - 1)])
        return softmax_pv(sb_ref[...], vl_ref[2 * j + 1], c)

    _, l, acc = lax.fori_loop(0, nt // 2, pair, carry)
    o_ref[...] = jnp.transpose(acc / l).astype(BF16)


def _attention(q_t, k_l, v_l, k_c, v_c, tq):
    b, nt, h, _, tn = q_t.shape
    n = nt * tn
    per = tn // tq
    tk = k_l.shape[3]
    tkc = k_c.shape[3]
    assert nt % 2 == 0, "key chunks are processed in pairs"
    return pl.pallas_call(
        _attn_kernel,
        grid=(b, h, n // tq),
        in_specs=[pl.BlockSpec((None, None, None, QK_DIM, tq), lambda bi, hi, qi: (bi, qi // per, hi, 0, qi % per)),
                  pl.BlockSpec((None, nt, None, tk, QK_DIM), lambda bi, hi, qi: (bi, 0, hi, 0, 0)),
                  pl.BlockSpec((None, 1, None, tkc, QK_DIM), lambda bi, hi, qi: (bi, 0, hi, 0, 0)),
                  pl.BlockSpec((None, nt, None, V_HEAD, tk), lambda bi, hi, qi: (bi, 0, hi, 0, 0)),
                  pl.BlockSpec((None, 1, None, V_HEAD, tkc), lambda bi, hi, qi: (bi, 0, hi, 0, 0))],
        out_specs=pl.BlockSpec((None, tq, V_HEAD), lambda bi, hi, qi: (bi, qi, hi)),
        out_shape=jax.ShapeDtypeStruct((b, n, h * V_HEAD), BF16),
        scratch_shapes=[pltpu.VMEM((tk, tq), F32), pltpu.VMEM((tk, tq), F32)],
        compiler_params=_cparams(("parallel", "parallel", "arbitrary")),
        name="attention",
    )(q_t, k_l, k_c, v_l, v_c)


_HALO = 16


def _conv_kernel(p_ref, x_ref, n_ref, w_ref, b_ref, o_ref):
    i = pl.program_id(1)
    x = x_ref[...].astype(F32)
    t = x.shape[0]
    prev = jnp.where(i > 0, p_ref[...].astype(F32), 0.0)
    nxt = jnp.where(i < pl.num_programs(1) - 1, n_ref[...].astype(F32), 0.0)
    full = jnp.concatenate([prev, x, nxt], axis=0)
    w = w_ref[...]
    acc = b_ref[...] + w[0:1] * full[_HALO - 2:_HALO - 2 + t]
    for k in range(1, SSD_CONV):
        acc = acc + w[k:k + 1] * full[_HALO - 2 + k:_HALO - 2 + k + t]
    o_ref[...] = _silu(acc).astype(BF16)


def _conv(u_main, conv_w, conv_b, tc):
    b, n, _ = u_main.shape
    nh = n // _HALO
    per = tc // _HALO

    def run(width, col0, w, bias, name):
        nblk = w.shape[1] // width
        c0 = col0 // width
        return pl.pallas_call(
            _conv_kernel,
            grid=(b, n // tc, nblk),
            in_specs=[pl.BlockSpec((None, _HALO, width), lambda bi, i, j: (bi, jnp.maximum(i * per - 1, 0), c0 + j)),
                      pl.BlockSpec((None, tc, width), lambda bi, i, j: (bi, i, c0 + j)),
                      pl.BlockSpec((None, _HALO, width), lambda bi, i, j: (bi, jnp.minimum((i + 1) * per, nh - 1), c0 + j)),
                      pl.BlockSpec((SSD_CONV, width), lambda bi, i, j: (0, j)),
                      pl.BlockSpec((1, width), lambda bi, i, j: (0, j))],
            out_specs=pl.BlockSpec((None, None, tc, width), lambda bi, i, j: (bi, j, i, 0)),
            out_shape=jax.ShapeDtypeStruct((b, nblk, n, width), BF16),
            compiler_params=_cparams(("parallel", "arbitrary", "arbitrary")),
            name=name,
        )(u_main, u_main, u_main, w, bias)

    gn = SSD_GROUPS * SSD_STATE
    bias = conv_b.reshape(1, -1)
    xg = run(GROUP_WIDTH, U_XBC, conv_w[:, :SSD_INNER], bias[:, :SSD_INNER], "conv_x")
    bg = run(SSD_STATE, U_XBC + SSD_INNER, conv_w[:, SSD_INNER:SSD_INNER + gn], bias[:, SSD_INNER:SSD_INNER + gn], "conv_b")
    cg = run(SSD_STATE, U_XBC + SSD_INNER + gn, conv_w[:, SSD_INNER + gn:], bias[:, SSD_INNER + gn:], "conv_c")
    return xg, bg, cg


def _dt_kernel(dt_ref, bias_ref, a_ref, dtb_ref, dob_ref, dsb_ref, cst_ref, dc_ref):
    q = dt_ref.shape[0]
    dt = jax.nn.softplus(dt_ref[...] + bias_ref[...])
    a = dt * a_ref[...]
    row = lax.broadcasted_iota(jnp.int32, a.shape, 0)
    cs = a
    s = 1
    while s < q:
        cs = cs + jnp.where(row >= s, pltpu.roll(cs, s, 0), 0.0)
        s *= 2
    total = cs[q - 1:q, :]
    lane = lax.broadcasted_iota(jnp.int32, a.shape, 1)
    cs = jnp.where(lane < SSD_HEADS, cs, total - cs + a)
    dtb_ref[...] = dt.astype(BF16)
    dob_ref[...] = jnp.exp(cs).astype(BF16)
    dsb_ref[...] = jnp.exp(total - cs).astype(BF16)
    cst_ref[...] = jnp.transpose(cs)
    dc_ref[...] = jnp.broadcast_to(jnp.exp(total), dc_ref.shape)


def _dt_prep(u_small, dt_bias, a_coef):
    b, n, _ = u_small.shape
    q = SSD_CHUNK
    nc = n // q
    vec = lambda: pl.BlockSpec((1, 128), lambda bi, c: (0, 0))
    tok = lambda: pl.BlockSpec((None, q, 128), lambda bi, c: (bi, c, 0))
    return pl.pallas_call(
        _dt_kernel,
        grid=(b, nc),
        in_specs=[tok(), vec(), vec()],
        out_specs=[tok(), tok(), tok(),
                   pl.BlockSpec((None, None, 128, q), lambda bi, c: (bi, c, 0, 0)),
                   pl.BlockSpec((None, None, 8, 128), lambda bi, c: (bi, c, 0, 0))],
        out_shape=[jax.ShapeDtypeStruct((b, n, 128), BF16)] * 3
        + [jax.ShapeDtypeStruct((b, nc, 128, q), F32), jax.ShapeDtypeStruct((b, nc, 8, 128), F32)],
        compiler_params=_cparams(("parallel", "parallel")),
        name="dt_prep",
    )(u_small, dt_bias, a_coef)


_TN = (((0,), (0,)), ((), ()))


def _ssd_kernel(x_ref, b_ref, c_ref, dt_ref, do_ref, ds_ref, cst_ref, dc_ref, e_ref, init_ref,
                y_ref, fin_ref, st_ref, *, direction, cpb):
    q = SSD_CHUNK
    step = pl.program_id(1)

    @pl.when(step == 0)
    def _():
        st_ref[...] = init_ref[...]

    ri = lax.broadcasted_iota(jnp.int32, (q, q), 0)
    ci = lax.broadcasted_iota(jnp.int32, (q, q), 1)
    mask = (ri >= ci) if direction == 0 else (ri <= ci)
    lane = lax.broadcasted_iota(jnp.int32, (q, 2 * SSD_HEADDIM), 1)
    left = lane < SSD_HEADDIM

    def chunk_body(cc, carry):
        c = cc if direction == 0 else cpb - 1 - cc
        rows = pl.ds(pl.multiple_of(c * q, q), q)
        dtb = dt_ref[rows, :]
        dob = do_ref[rows, :]
        dsb = ds_ref[rows, :]

        def group_body(g, carry2):
            eg = e_ref[g]
            dtx = jnp.dot(dtb, eg, preferred_element_type=F32)
            dox = jnp.dot(dob, eg, preferred_element_type=F32)
            dsx = jnp.dot(dsb, eg, preferred_element_type=F32)
            xdt = x_ref[g, rows, :].astype(F32) * dtx
            xdt_b = xdt.astype(BF16)
            xds_b = (xdt * dsx).astype(BF16)
            bg = b_ref[g, rows, :]
            cg = c_ref[g, rows, :]
            cb = lax.dot_general(cg, bg, _NT, preferred_element_type=F32)
            st = st_ref[g]
            y_off = jnp.dot(cg, st.astype(BF16), preferred_element_type=F32) * dox
            g8 = cst_ref[c, pl.ds(pl.multiple_of(direction * SSD_HEADS + g * HEADS_PER_GROUP, 8), 8), :]
            g8t = jnp.transpose(g8)
            pieces = []
            for pr in range(HEADS_PER_GROUP // 2):
                pair = xdt_b[:, pr * 128:(pr + 1) * 128]
                acc = None
                for half in range(2):
                    r = 2 * pr + half
                    seg = g8t[:, r:r + 1] - g8[r:r + 1, :]
                    m = (cb * jnp.exp(jnp.where(mask, seg, -jnp.inf))).astype(BF16)
                    xin = jnp.where(left if half == 0 else jnp.logical_not(left), pair, jnp.zeros_like(pair))
                    d = jnp.dot(m, xin, preferred_element_type=F32)
                    acc = d if acc is None else acc + d
                pieces.append(acc)
            y = jnp.concatenate(pieces, axis=1) + y_off
            y_ref[g, rows, :] = y.astype(BF16)
            st_new = lax.dot_general(bg, xds_b, _TN, preferred_element_type=F32)
            st_ref[g] = st * dc_ref[c, pl.ds(g, 1), :] + st_new
            return carry2

        lax.fori_loop(0, SSD_GROUPS, group_body, 0)
        return carry

    lax.fori_loop(0, cpb, chunk_body, 0)

    @pl.when(step == pl.num_programs(1) - 1)
    def _():
        fin_ref[...] = st_ref[...]


def _ssd(xg, bg, cg, dtb, dob, dsb, cst, dcx, e_mat, init, direction, cpb):
    b, g, n, _ = xg.shape
    q = SSD_CHUNK
    t = q * cpb
    nt = n // t
    ti = (lambda i: i) if direction == 0 else (lambda i: nt - 1 - i)
    grp = lambda w: pl.BlockSpec((None, g, t, w), lambda bi, i: (bi, 0, ti(i), 0))
    tok = lambda: pl.BlockSpec((None, t, 128), lambda bi, i: (bi, ti(i), 0))
    return pl.pallas_call(
        functools.partial(_ssd_kernel, direction=direction, cpb=cpb),
        grid=(b, nt),
        in_specs=[grp(GROUP_WIDTH), grp(SSD_STATE), grp(SSD_STATE), tok(), tok(), tok(),
                  pl.BlockSpec((None, cpb, 128, q), lambda bi, i: (bi, ti(i), 0, 0)),
                  pl.BlockSpec((None, cpb, None, g, GROUP_WIDTH), lambda bi, i: (bi, ti(i), direction, 0, 0)),
                  pl.BlockSpec((None, g, 128, GROUP_WIDTH), lambda bi, i: (direction, 0, 0, 0)),
                  pl.BlockSpec((None, g, SSD_STATE, GROUP_WIDTH), lambda bi, i: (bi, 0, 0, 0))],
        out_specs=[grp(GROUP_WIDTH),
                   pl.BlockSpec((None, g, SSD_STATE, GROUP_WIDTH), lambda bi, i: (bi, 0, 0, 0))],
        out_shape=[jax.ShapeDtypeStruct((b, g, n, GROUP_WIDTH), BF16),
                   jax.ShapeDtypeStruct((b, g, SSD_STATE, GROUP_WIDTH), F32)],
        scratch_shapes=[pltpu.VMEM((g, SSD_STATE, GROUP_WIDTH), F32)],
        compiler_params=_cparams(("parallel", "arbitrary")),
        name="ssd_fwd" if direction == 0 else "ssd_bwd",
    )(xg, bg, cg, dtb, dob, dsb, cst, dcx, e_mat, init)


def _ssd_out_kernel(x_ref, yf_ref, yb_ref, z_ref, d_ref, g_ref, w_ref, o_ref):
    ys = []
    ss = None
    for g in range(SSD_GROUPS):
        cols = slice(g * GROUP_WIDTH, (g + 1) * GROUP_WIDTH)
        y = d_ref[:, cols] * x_ref[g].astype(F32) + yf_ref[g].astype(F32) + yb_ref[g].astype(F32)
        y = y * _silu(z_ref[:, cols].astype(F32))
        sq = jnp.sum(y * y, axis=-1, keepdims=True)
        ss = sq if ss is None else ss + sq
        ys.append(y)
    inv = lax.rsqrt(ss * (1.0 / SSD_INNER) + NORM_EPS)
    acc = None
    for g in range(SSD_GROUPS):
        cols = slice(g * GROUP_WIDTH, (g + 1) * GROUP_WIDTH)
        yn = (ys[g] * inv * g_ref[:, cols]).astype(BF16)
        d = jnp.dot(yn, w_ref[cols, :], preferred_element_type=F32)
        acc = d if acc is None else acc + d
    o_ref[...] = acc.astype(BF16)


def _ssd_out(xg, yf, yb, u_main, d_exp, g_ssd, w, tm):
    b, g, n, gw = xg.shape
    dm = w.shape[1]
    grp = lambda: pl.BlockSpec((None, g, tm, gw), lambda bi, i: (bi, 0, i, 0))
    return pl.pallas_call(
        _ssd_out_kernel,
        grid=(b, n // tm),
        in_specs=[grp(), grp(), grp(),
                  pl.BlockSpec((None, tm, SSD_INNER), lambda bi, i: (bi, i, U_Z // SSD_INNER)),
                  pl.BlockSpec((1, SSD_INNER), lambda bi, i: (0, 0)),
                  pl.BlockSpec((1, SSD_INNER), lambda bi, i: (0, 0)),
                  pl.BlockSpec((SSD_INNER, dm), lambda bi, i: (0, 0))],
        out_specs=pl.BlockSpec((None, tm, dm), lambda bi, i: (bi, i, 0)),
        out_shape=jax.ShapeDtypeStruct((b, n, dm), BF16),
        compiler_params=_cparams(("parallel", "parallel")),
        name="ssd_out",
    )(xg, yf, yb, u_main, d_exp, g_ssd, w)


def _pack_pairs(x):
    k = x.shape[1] // 2
    u = lax.bitcast_convert_type(x.astype(BF16).astype(F32), U32)
    return (u[:, k:] & jnp.uint32(0xFFFF0000)) | (u[:, :k] >> 16)


def _unpack_pairs(u):
    lo = lax.bitcast_convert_type(u << 16, F32)
    hi = lax.bitcast_convert_type(u & jnp.uint32(0xFFFF0000), F32)
    return jnp.concatenate([lo, hi], axis=1)


def _merge_kernel(s_ref, o_ref, gate_ref, x_ref, ga_ref, shf_ref, scf_ref, gpost_ref, gpre_ref,
                  wo_ref, wout_ref, wrh_ref, wrl_ref, br_ref, x1_ref, hp_ref, lg_ref):
    d = x_ref.shape[1]
    y_mla = jnp.dot(o_ref[...], wo_ref[...], preferred_element_type=F32)
    gates = gate_ref[...].astype(F32)
    mix = jax.nn.sigmoid(gates[:, :d]) * s_ref[...].astype(F32) + jax.nn.sigmoid(gates[:, d:]) * y_mla
    ym = jnp.dot(mix.astype(BF16), wout_ref[...], preferred_element_type=F32)
    x1 = x_ref[...] + ga_ref[...] * (_rms(ym) * gpost_ref[...])
    x1_ref[...] = x1
    h2 = _rms(x1) * (gpre_ref[...] * (1.0 + scf_ref[...])) + shf_ref[...]
    hp_ref[...] = _pack_pairs(h2)
    hh = h2.astype(BF16)
    hl = (h2 - hh.astype(F32)).astype(BF16)
    lg = (jnp.dot(hh, wrh_ref[...], preferred_element_type=F32)
          + jnp.dot(hh, wrl_ref[...], preferred_element_type=F32)
          + jnp.dot(hl, wrh_ref[...], preferred_element_type=F32))
    lg_ref[...] = lg + br_ref[...]


def _merge(s, o, u_main, x, g_a, sh_f, sc_f, g_post, g_pre, w_o, w_out, wr_hi, wr_lo, b_r, tm):
    b, n, d = x.shape
    tokd = lambda: pl.BlockSpec((None, tm, d), lambda bi, i: (bi, i, 0))
    modv = lambda: pl.BlockSpec((None, 1, d), lambda bi, i: (bi, 0, 0))
    vec = lambda w: pl.BlockSpec((1, w), lambda bi, i: (0, 0))
    mat = lambda r, c: pl.BlockSpec((r, c), lambda bi, i: (0, 0))
    return pl.pallas_call(
        _merge_kernel,
        grid=(b, n // tm),
        in_specs=[tokd(), tokd(),
                  pl.BlockSpec((None, tm, 2 * d), lambda bi, i: (bi, i, U_GATE // (2 * d))),
                  tokd(), modv(), modv(), modv(), vec(d), vec(d),
                  mat(d, d), mat(d, d), mat(d, 128), mat(d, 128), vec(128)],
        out_specs=[tokd(),
                   pl.BlockSpec((None, tm, d // 2), lambda bi, i: (bi, i, 0)),
                   pl.BlockSpec((None, tm, 128), lambda bi, i: (bi, i, 0))],
        out_shape=[jax.ShapeDtypeStruct((b, n, d), F32),
                   jax.ShapeDtypeStruct((b, n, d // 2), U32),
                   jax.ShapeDtypeStruct((b, n, 128), F32)],
        compiler_params=_cparams(("parallel", "parallel")),
        name="merge",
    )(s, o, u_main, x, g_a, sh_f, sc_f, g_post, g_pre, w_o, w_out, wr_hi, wr_lo, b_r)


_GATHER_UNROLL = 8


def _row_copy(src_ref, dst_ref, sem, src_row, dst_row):
    return pltpu.make_async_copy(src_ref.at[pl.ds(src_row, 1), :], dst_ref.at[pl.ds(dst_row, 1), :], sem)


def _gather_start(idx_ref, base, src_ref, dst_ref, sem):
    def issue(r, carry):
        _row_copy(src_ref, dst_ref, sem, idx_ref[base + r], r).start()
        return carry

    lax.fori_loop(0, dst_ref.shape[0], issue, 0, unroll=_GATHER_UNROLL)


def _gather_wait(src_ref, dst_ref, sem):
    def drain(r, carry):
        _row_copy(src_ref, dst_ref, sem, 0, r).wait()
        return carry

    lax.fori_loop(0, dst_ref.shape[0], drain, 0, unroll=_GATHER_UNROLL)


def _expert_kernel(be_ref, nu_ref, tid_ref, h_ref, wg_ref, wu_ref, wd_ref, y_ref, x_buf, sem):
    i = pl.program_id(0)

    @pl.when(i < nu_ref[0])
    def _():
        _gather_start(tid_ref, i * MOE_BLOCK, h_ref, x_buf, sem)
        _gather_wait(h_ref, x_buf, sem)
        x = _unpack_pairs(x_buf[...]).astype(BF16)
        hg = jnp.dot(x, wg_ref[...], preferred_element_type=F32)
        hu = jnp.dot(x, wu_ref[...], preferred_element_type=F32)
        a = (_silu(hg) * hu).astype(BF16)
        y_ref[...] = _pack_pairs(jnp.dot(a, wd_ref[...], preferred_element_type=F32))

    @pl.when(i >= nu_ref[0])
    def _():
        y_ref[...] = jnp.zeros(y_ref.shape, y_ref.dtype)


def _experts(h_pack, tid_buf, blk_e, n_used, w_gate, w_up, w_down):
    rows = tid_buf.shape[0]
    dh = h_pack.shape[1]
    ne, d, ff = w_gate.shape
    nb = rows // MOE_BLOCK
    return pl.pallas_call(
        _expert_kernel,
        grid_spec=pltpu.PrefetchScalarGridSpec(
            num_scalar_prefetch=3,
            grid=(nb,),
            in_specs=[pl.BlockSpec(memory_space=pl.ANY),
                      pl.BlockSpec((None, d, ff), lambda i, be, nu, tid: (be[i], 0, 0)),
                      pl.BlockSpec((None, d, ff), lambda i, be, nu, tid: (be[i], 0, 0)),
                      pl.BlockSpec((None, ff, d), lambda i, be, nu, tid: (be[i], 0, 0))],
            out_specs=pl.BlockSpec((MOE_BLOCK, dh), lambda i, be, nu, tid: (i, 0)),
            scratch_shapes=[pltpu.VMEM((MOE_BLOCK, dh), U32), pltpu.SemaphoreType.DMA(())]),
        out_shape=jax.ShapeDtypeStruct((rows, dh), U32),
        compiler_params=_cparams(("arbitrary",)),
        name="experts",
    )(blk_e, n_used, tid_buf, h_pack, w_gate, w_up, w_down)


def _final_kernel(p0_ref, p1_ref, y_ref, w_ref, x1_ref, gf_ref, gpost_ref, o_ref, y0_buf, y1_buf, sem0, sem1):
    tm = y0_buf.shape[0]
    base = (pl.program_id(0) * pl.num_programs(1) + pl.program_id(1)) * tm
    _gather_start(p0_ref, base, y_ref, y0_buf, sem0)
    _gather_start(p1_ref, base, y_ref, y1_buf, sem1)
    _gather_wait(y_ref, y0_buf, sem0)
    _gather_wait(y_ref, y1_buf, sem1)
    w = w_ref[...]
    f = w[:, 0:1] * _unpack_pairs(y0_buf[...]) + w[:, 1:2] * _unpack_pairs(y1_buf[...])
    o_ref[...] = x1_ref[...] + gf_ref[...] * (_rms(f) * gpost_ref[...])


def _final(y_buf, pos0, pos1, wts, x1, g_f, g_post, tm):
    b, n, d = x1.shape
    return pl.pallas_call(
        _final_kernel,
        grid_spec=pltpu.PrefetchScalarGridSpec(
            num_scalar_prefetch=2,
            grid=(b, n // tm),
            in_specs=[pl.BlockSpec(memory_space=pl.ANY),
                      pl.BlockSpec((None, tm, 128), lambda bi, i, p0, p1: (bi, i, 0)),
                      pl.BlockSpec((None, tm, d), lambda bi, i, p0, p1: (bi, i, 0)),
                      pl.BlockSpec((None, 1, d), lambda bi, i, p0, p1: (bi, 0, 0)),
                      pl.BlockSpec((1, d), lambda bi, i, p0, p1: (0, 0))],
            out_specs=pl.BlockSpec((None, tm, d), lambda bi, i, p0, p1: (bi, i, 0)),
            scratch_shapes=[pltpu.VMEM((tm, d // 2), U32), pltpu.VMEM((tm, d // 2), U32),
                            pltpu.SemaphoreType.DMA(()), pltpu.SemaphoreType.DMA(())]),
        out_shape=jax.ShapeDtypeStruct((b, n, d), F32),
        compiler_params=_cparams(("arbitrary", "arbitrary")),
        name="final",
    )(pos0, pos1, y_buf, wts, x1, g_f, g_post)


def _route(logits):
    t = logits.shape[0]
    g_prob = jax.nn.softmax(logits[:, :N_GROUPS_E], axis=-1)
    g_sel = jnp.argmax(g_prob, axis=-1)
    g_top = jnp.max(g_prob, axis=-1)
    e_logits = logits[:, N_GROUPS_E:N_GROUPS_E + N_EXPERTS].reshape(t, N_GROUPS_E, EXPERTS_PER_GROUP)
    e_logits = jnp.take_along_axis(e_logits, g_sel[:, None, None], axis=1)[:, 0]
    e_top, e_sel = lax.top_k(jax.nn.softmax(e_logits, axis=-1), TOP_K)
    weights = g_top[:, None] * e_top / jnp.sum(e_top, axis=-1, keepdims=True)
    expert_id = (g_sel[:, None] * EXPERTS_PER_GROUP + e_sel).astype(jnp.int32)
    return expert_id, weights


def _dispatch_plan(expert_id):
    t = expert_id.shape[0]
    n_assign = t * TOP_K
    eid = expert_id.reshape(n_assign)
    onehot = (eid[:, None] == jnp.arange(N_EXPERTS, dtype=jnp.int32)[None, :]).astype(jnp.int32)
    csum = jnp.cumsum(onehot, axis=0)
    counts = csum[-1]
    rank = jnp.take_along_axis(csum, eid[:, None], axis=1)[:, 0] - 1
    start = jnp.cumsum(counts) - counts
    nblk_e = (counts + MOE_BLOCK - 1) // MOE_BLOCK
    bend = jnp.cumsum(nblk_e)
    pstart = (bend - nblk_e) * MOE_BLOCK
    pos = (pstart[eid] + rank).astype(jnp.int32)
    n_blocks = n_assign // MOE_BLOCK + N_EXPERTS
    n_used = bend[-1].astype(jnp.int32)
    blk = jnp.arange(n_blocks, dtype=jnp.int32)
    blk_e = jnp.searchsorted(bend, jnp.minimum(blk, n_used - 1), side='right').astype(jnp.int32)
    blk_e = jnp.clip(blk_e, 0, N_EXPERTS - 1)
    order = jnp.argsort(eid, stable=True).astype(jnp.int32)
    rowi = jnp.arange(n_blocks * MOE_BLOCK, dtype=jnp.int32)
    re = blk_e[rowi // MOE_BLOCK]
    off = rowi - pstart[re]
    valid = (off < counts[re]) & (rowi // MOE_BLOCK < n_used)
    src = jnp.clip(start[re] + off, 0, n_assign - 1)
    tid_buf = jnp.where(valid, order[src] // TOP_K, 0).astype(jnp.int32)
    return pos, tid_buf, blk_e, n_used.reshape(1)


def _rope_tables(n):
    rows = n // GRID_W
    row = jnp.repeat(jnp.arange(rows), GRID_W).astype(F32)
    col = jnp.tile(jnp.arange(GRID_W), rows).astype(F32)
    half = QK_ROPE // 2
    freqs = ROPE_THETA ** (-jnp.arange(0, half, 2, dtype=F32) / half)
    ar = row[:, None] * freqs
    ac = col[:, None] * freqs
    cos = jnp.concatenate([jnp.cos(ar), jnp.cos(ar), jnp.cos(ac), jnp.cos(ac)], axis=1)
    sin = jnp.concatenate([-jnp.sin(ar), jnp.sin(ar), -jnp.sin(ac), jnp.sin(ac)], axis=1)
    return cos.T, sin.T, cos, sin


def _pick(n, pref):
    return pref if n % pref == 0 else n


def kernel(x, c, ctx, c_ctx, w_ada, b_ada, g_pre_mix, g_post_mix, g_pre_ffn, g_post_ffn, w_in, g_q, w_uq,
           g_kv, w_ukv, w_o_mla, conv_w, conv_b, a_log, dt_bias, d_skip, g_ssd, w_ssd_out, w_out,
           w_router_group, b_router_group, w_router_expert, b_router_expert, w_exp_gate, w_exp_up, w_exp_down):
    assert w_ada.shape[0] == 1, "single-layer block"
    b, n, d = x.shape
    n_ctx = ctx.shape[1]
    h = MLA_HEADS

    w_in0 = w_in[0]
    o_ckv = Q_LORA
    o_kr = o_ckv + KV_LORA
    o_z = o_kr + QK_ROPE
    o_xbc = o_z + SSD_INNER
    o_dt = o_xbc + SSD_INNER + 2 * SSD_GROUPS * SSD_STATE
    o_gate = o_dt + 2 * SSD_HEADS
    w_kr = w_in0[:, o_kr:o_z]
    swap = jnp.concatenate([jnp.arange(16, 32), jnp.arange(0, 16), jnp.arange(48, 64), jnp.arange(32, 48)])
    w_main = jnp.concatenate([w_in0[:, o_z:o_xbc], w_in0[:, o_gate:], w_in0[:, o_xbc:o_dt],
                              w_in0[:, :o_ckv], w_in0[:, o_ckv:o_kr]], axis=1).astype(BF16)
    w_small = jnp.concatenate([w_in0[:, o_dt:o_gate], w_kr, w_kr[:, swap]], axis=1).astype(BF16)
    wq_t = w_uq[0].reshape(Q_LORA, h, QK_DIM).transpose(1, 2, 0).astype(BF16)
    wkv = w_ukv[0].reshape(KV_LORA, h, QK_NOPE + V_HEAD)
    wk = wkv[:, :, :QK_NOPE].transpose(1, 0, 2).astype(BF16)
    wv_t = wkv[:, :, QK_NOPE:].transpose(1, 2, 0).astype(BF16)
    w_o = w_o_mla[0].astype(BF16)
    w_so = w_ssd_out[0].astype(BF16)
    w_mix = w_out[0].astype(BF16)
    w_r = jnp.concatenate([w_router_group[0], w_router_expert[0],
                           jnp.zeros((d, 128 - N_GROUPS_E - N_EXPERTS), F32)], axis=1)
    wr_hi = w_r.astype(BF16)
    wr_lo = (w_r - wr_hi.astype(F32)).astype(BF16)
    b_r = jnp.concatenate([b_router_group[0], b_router_expert[0],
                           jnp.zeros((128 - N_GROUPS_E - N_EXPERTS,), F32)]).reshape(1, 128)
    wg = w_exp_gate[0].astype(BF16)
    wu = w_exp_up[0].astype(BF16)
    wd = w_exp_down[0].astype(BF16)
    vec = lambda v: v.reshape(1, -1)

    c8 = jnp.concatenate([c, c_ctx[None, :], jnp.zeros((8 - b - 1, d), F32)], axis=0)
    mod = _ada(c8, w_ada[0], b_ada[0])
    sh_a, sc_a, g_a, sh_f, sc_f, g_f = [mod[:, i * d:(i + 1) * d] for i in range(N_MOD)]
    lat = lambda m: m[:b].reshape(b, 1, d)
    cmod = lambda m: jnp.broadcast_to(m[b:b + 1].reshape(1, 1, d), (b, 1, d))

    u_main, u_small = _inproj(x, lat(sc_a), lat(sh_a), vec(g_pre_mix[0]), w_main, w_small, _pick(n, 512))
    uc_main, uc_small = _inproj(ctx, cmod(sc_a), cmod(sh_a), vec(g_pre_mix[0]), w_main, w_small, n_ctx)

    tn = _pick(n, 512)
    q_t, k_l, v_l = _qkv(u_main, u_small, vec(g_q[0]), vec(g_kv[0]), wq_t, wk, wv_t, _rope_tables(n), tn)
    k_c, v_c = _qkv(uc_main, uc_small, None, vec(g_kv[0]), None, wk, wv_t, None, n_ctx)
    o_mla = _attention(q_t, k_l, v_l, k_c, v_c, tn)

    a_coef = -jnp.exp(a_log[0].astype(F32)).reshape(1, 2 * SSD_HEADS)
    dtb_flat = dt_bias[0].astype(F32).reshape(1, 2 * SSD_HEADS)
    hh = jnp.arange(2 * SSD_HEADS)
    lane_head = jnp.arange(GROUP_WIDTH) // SSD_HEADDIM
    e_mat = (hh[None, None, :, None] == (jnp.arange(2)[:, None, None, None] * SSD_HEADS
                                        + jnp.arange(SSD_GROUPS)[None, :, None, None] * HEADS_PER_GROUP
                                        + lane_head[None, None, None, :])).astype(BF16)

    def ssd_inputs(um, us, tc):
        xg, bg, cg = _conv(um, conv_w[0], conv_b[0], tc)
        dtb, dob, dsb, cst, dc = _dt_prep(us, dtb_flat, a_coef)
        nc = dc.shape[1]
        dcx = jnp.repeat(dc[:, :, 0, :].reshape(b, nc, 2, SSD_GROUPS, HEADS_PER_GROUP), SSD_HEADDIM, axis=-1)
        return xg, bg, cg, dtb, dob, dsb, cst, dcx

    zero_state = jnp.zeros((b, SSD_GROUPS, SSD_STATE, GROUP_WIDTH), F32)
    ctx_in = ssd_inputs(uc_main, uc_small, n_ctx)
    lat_in = ssd_inputs(u_main, u_small, _pick(n, 256))
    cpb_c = n_ctx // SSD_CHUNK
    cpb = _pick(n, 512) // SSD_CHUNK
    ys = []
    for direction in range(2):
        _, s0 = _ssd(*ctx_in, e_mat, zero_state, direction, cpb_c)
        y_dir, _ = _ssd(*lat_in, e_mat, s0, direction, cpb)
        ys.append(y_dir)
    d_exp = jnp.repeat(d_skip[0].astype(F32), SSD_HEADDIM).reshape(1, SSD_INNER)
    s_l = _ssd_out(lat_in[0], ys[0], ys[1], u_main, d_exp, vec(g_ssd[0]), w_so, _pick(n, 256))

    x1, h_pack, logits = _merge(s_l, o_mla, u_main, x, lat(g_a), lat(sh_f), lat(sc_f), vec(g_post_mix[0]),
                                vec(g_pre_ffn[0]), w_o, w_mix, wr_hi, wr_lo, b_r, _pick(n, 256))

    t = b * n
    expert_id, weights = _route(logits.reshape(t, 128))
    pos, tid_buf, blk_e, n_used = _dispatch_plan(expert_id)
    y_buf = _experts(h_pack.reshape(t, d // 2), tid_buf, blk_e, n_used, wg, wu, wd)
    pos = pos.reshape(t, TOP_K)
    wts = jnp.concatenate([weights, jnp.zeros((t, 128 - TOP_K), F32)], axis=1).reshape(b, n, 128)
    return _final(y_buf, pos[:, 0], pos[:, 1], wts, x1, lat(g_f), vec(g_post_ffn[0]), _pick(n, 256))
```

```python
import functools
import math

import jax
import jax.numpy as jnp
from jax import lax
from jax.experimental import pallas as pl
from jax.experimental.pallas import tpu as pltpu

F32 = jnp.float32
BF16 = jnp.bfloat16
U32 = jnp.uint32

GRID_W = 64
N_MOD = 6
NORM_EPS = 1e-6
MLA_HEADS = 16
Q_LORA = 512
KV_LORA = 512
QK_NOPE = 128
QK_ROPE = 64
QK_DIM = QK_NOPE + QK_ROPE
V_HEAD = 128
ROPE_THETA = 10000.0
ATTN_SCALE = QK_DIM ** -0.5
SSD_HEADDIM = 64
SSD_HEADS = 64
SSD_INNER = SSD_HEADS * SSD_HEADDIM
SSD_GROUPS = 8
SSD_STATE = 128
SSD_CONV = 5
HEADS_PER_GROUP = SSD_HEADS // SSD_GROUPS
GROUP_WIDTH = HEADS_PER_GROUP * SSD_HEADDIM
N_GROUPS_E = 8
EXPERTS_PER_GROUP = 8
N_EXPERTS = N_GROUPS_E * EXPERTS_PER_GROUP
TOP_K = 2
LOG2E = 1.4426950408889634

U_XBC = 0
U_Z = U_XBC + SSD_INNER + 2 * SSD_GROUPS * SSD_STATE
U_GATE = U_Z + SSD_INNER
U_CQ = U_GATE + 2 * 2048
U_CKV = U_CQ + Q_LORA
U_MAIN = U_CKV + KV_LORA
HALF_INNER = SSD_INNER // 2
U_SMALL = 256

SSD_CHUNK = 128
MOE_BLOCK = 256
VMEM_LIMIT = 56 * 1024 * 1024


def _cparams(sem):
    return pltpu.CompilerParams(dimension_semantics=sem, vmem_limit_bytes=VMEM_LIMIT)


def _silu(x):
    return x * jax.nn.sigmoid(x)


def _rms(x, eps=NORM_EPS):
    return x * lax.rsqrt(jnp.mean(x * x, axis=-1, keepdims=True) + eps)


def _ada_kernel(c_ref, w_ref, b_ref, o_ref):
    c = c_ref[...]
    o_ref[...] = jnp.dot(_silu(c).astype(BF16), w_ref[...].astype(BF16),
                         preferred_element_type=F32) + b_ref[...]


def _ada(c8, w_ada, b_ada):
    d, n = w_ada.shape
    tn = 1536
    return pl.pallas_call(
        _ada_kernel,
        grid=(n // tn,),
        in_specs=[pl.BlockSpec((8, d), lambda j: (0, 0)),
                  pl.BlockSpec((d, tn), lambda j: (0, j)),
                  pl.BlockSpec((1, tn), lambda j: (0, j))],
        out_specs=pl.BlockSpec((8, tn), lambda j: (0, j)),
        out_shape=jax.ShapeDtypeStruct((8, n), F32),
        compiler_params=_cparams(("parallel",)),
        name="ada",
    )(c8, w_ada, b_ada.reshape(1, n))


def _inproj_kernel(x_ref, sc_ref, sh_ref, g_ref, wm_ref, ws_ref, om_ref, os_ref, h_scr):
    @pl.when(pl.program_id(2) == 0)
    def _():
        y = _rms(x_ref[...]) * (g_ref[...] * (1.0 + sc_ref[...])) + sh_ref[...]
        hb = y.astype(BF16)
        h_scr[...] = hb
        os_ref[...] = jnp.dot(hb, ws_ref[...], preferred_element_type=F32)

    om_ref[...] = jnp.dot(h_scr[...], wm_ref[...], preferred_element_type=F32).astype(BF16)


def _inproj(x, scale, shift, g, w_main, w_small, tm):
    b, n, d = x.shape
    tn = 1536
    return pl.pallas_call(
        _inproj_kernel,
        grid=(b, n // tm, U_MAIN // tn),
        in_specs=[pl.BlockSpec((None, tm, d), lambda bi, i, j: (bi, i, 0)),
                  pl.BlockSpec((None, 1, d), lambda bi, i, j: (bi, 0, 0)),
                  pl.BlockSpec((None, 1, d), lambda bi, i, j: (bi, 0, 0)),
                  pl.BlockSpec((1, d), lambda bi, i, j: (0, 0)),
                  pl.BlockSpec((d, tn), lambda bi, i, j: (0, j)),
                  pl.BlockSpec((d, U_SMALL), lambda bi, i, j: (0, 0))],
        out_specs=[pl.BlockSpec((None, tm, tn), lambda bi, i, j: (bi, i, j)),
                   pl.BlockSpec((None, tm, U_SMALL), lambda bi, i, j: (bi, i, 0))],
        out_shape=[jax.ShapeDtypeStruct((b, n, U_MAIN), BF16),
                   jax.ShapeDtypeStruct((b, n, U_SMALL), F32)],
        scratch_shapes=[pltpu.VMEM((tm, d), BF16)],
        compiler_params=_cparams(("parallel", "parallel", "arbitrary")),
        name="in_proj",
    )(x, scale, shift, g, w_main, w_small)


_NT = (((1,), (1,)), ((), ()))


def _qkv_kernel(*refs, rope, with_q):
    if rope:
        (cq_ref, ckv_ref, kr_ref, gq_ref, gkv_ref, wq_ref, wk_ref, wv_ref,
         cost_ref, sint_ref, cos_ref, sin_ref, q_ref, k_ref, v_ref) = refs
    else:
        ckv_ref, kr_ref, gkv_ref, wk_ref, wv_ref, k_ref, v_ref = refs
    ckvn = (_rms(ckv_ref[...].astype(F32)) * gkv_ref[...]).astype(BF16)
    kr = kr_ref[...]
    kpe = kr[:, :QK_ROPE]
    if rope:
        kpe = kpe * cos_ref[...] + kr[:, QK_ROPE:] * sin_ref[...]
    kpe = kpe.astype(BF16)
    if with_q:
        cqn = (_rms(cq_ref[...].astype(F32)) * gq_ref[...]).astype(BF16)
        cost = cost_ref[...]
        sint = sint_ref[...]

    def head(h, carry):
        if with_q:
            qt = lax.dot_general(wq_ref[h], cqn, _NT, preferred_element_type=F32)
            qp = qt[QK_NOPE:]
            qsw = jnp.concatenate([qp[16:32], qp[0:16], qp[48:64], qp[32:48]], axis=0)
            qp = qp * cost + qsw * sint
            q_ref[h, :QK_NOPE, :] = (qt[:QK_NOPE] * (ATTN_SCALE * LOG2E)).astype(BF16)
            q_ref[h, QK_NOPE:, :] = (qp * (ATTN_SCALE * LOG2E)).astype(BF16)
        kn = jnp.dot(ckvn, wk_ref[h], preferred_element_type=F32)
        k_ref[h, :, :QK_NOPE] = kn.astype(BF16)
        k_ref[h, :, QK_NOPE:] = kpe
        vt = lax.dot_general(wv_ref[h], ckvn, _NT, preferred_element_type=F32)
        v_ref[h] = vt.astype(BF16)
        return carry

    lax.fori_loop(0, MLA_HEADS, head, 0)


def _qkv(u_main, u_small, g_q, g_kv, wq_t, wk, wv_t, tables, tn):
    b, n, _ = u_main.shape
    nt = n // tn
    h = MLA_HEADS
    rope = tables is not None
    in_specs = []
    args = []
    if rope:
        in_specs.append(pl.BlockSpec((None, tn, Q_LORA), lambda bi, i: (bi, i, U_CQ // Q_LORA)))
        args.append(u_main)
    in_specs += [pl.BlockSpec((None, tn, KV_LORA), lambda bi, i: (bi, i, U_CKV // KV_LORA)),
                 pl.BlockSpec((None, tn, 128), lambda bi, i: (bi, i, 1))]
    args += [u_main, u_small]
    if rope:
        in_specs.append(pl.BlockSpec((1, Q_LORA), lambda bi, i: (0, 0)))
        args.append(g_q)
    in_specs.append(pl.BlockSpec((1, KV_LORA), lambda bi, i: (0, 0)))
    args.append(g_kv)
    if rope:
        in_specs.append(pl.BlockSpec((h, QK_DIM, Q_LORA), lambda bi, i: (0, 0, 0)))
        args.append(wq_t)
    in_specs += [pl.BlockSpec((h, KV_LORA, QK_NOPE), lambda bi, i: (0, 0, 0)),
                 pl.BlockSpec((h, V_HEAD, KV_LORA), lambda bi, i: (0, 0, 0))]
    args += [wk, wv_t]
    out_specs = []
    out_shape = []
    if rope:
        cos_t, sin_t, cos, sin = tables
        in_specs += [pl.BlockSpec((QK_ROPE, tn), lambda bi, i: (0, i)),
                     pl.BlockSpec((QK_ROPE, tn), lambda bi, i: (0, i)),
                     pl.BlockSpec((tn, QK_ROPE), lambda bi, i: (i, 0)),
                     pl.BlockSpec((tn, QK_ROPE), lambda bi, i: (i, 0))]
        args += [cos_t, sin_t, cos, sin]
        out_specs.append(pl.BlockSpec((None, None, h, QK_DIM, tn), lambda bi, i: (bi, i, 0, 0, 0)))
        out_shape.append(jax.ShapeDtypeStruct((b, nt, h, QK_DIM, tn), BF16))
    out_specs += [pl.BlockSpec((None, None, h, tn, QK_DIM), lambda bi, i: (bi, i, 0, 0, 0)),
                  pl.BlockSpec((None, None, h, V_HEAD, tn), lambda bi, i: (bi, i, 0, 0, 0))]
    out_shape += [jax.ShapeDtypeStruct((b, nt, h, tn, QK_DIM), BF16),
                  jax.ShapeDtypeStruct((b, nt, h, V_HEAD, tn), BF16)]
    return pl.pallas_call(
        functools.partial(_qkv_kernel, rope=rope, with_q=rope),
        grid=(b, nt),
        in_specs=in_specs,
        out_specs=out_specs,
        out_shape=out_shape,
        compiler_params=_cparams(("parallel", "parallel")),
        name="qkv_rope" if rope else "kv_ctx",
    )(*args)


def _attn_kernel(q_ref, kl_ref, kc_ref, vl_ref, vc_ref, o_ref, sa_ref, sb_ref):
    qt = q_ref[...]
    tq = qt.shape[1]
    nt = kl_ref.shape[0]

    def scores(kb):
        return jnp.dot(kb, qt, preferred_element_type=F32)

    def softmax_pv(s, vb, carry):
        m, l, acc = carry
        m_new = jnp.maximum(m, jnp.max(s, axis=0, keepdims=True))
        alpha = jnp.exp2(m - m_new)
        p = jnp.exp2(s - m_new)
        l = alpha * l + jnp.sum(p, axis=0, keepdims=True)
        acc = alpha * acc + jnp.dot(vb, p.astype(BF16), preferred_element_type=F32)
        return m_new, l, acc

    init = (jnp.full((1, tq), -jnp.inf, F32), jnp.zeros((1, tq), F32), jnp.zeros((V_HEAD, tq), F32))
    sa_ref[...] = scores(kl_ref[0])
    carry = softmax_pv(scores(kc_ref[0]), vc_ref[0], init)

    def pair(j, c):
        sb_ref[...] = scores(kl_ref[2 * j + 1])
        c = softmax_pv(sa_ref[...], vl_ref[2 * j], c)
        sa_ref[...] = scores(kl_ref[jnp.minimum(2 * j + 2, nt - 1)])
        return softmax_pv(sb_ref[...], vl_ref[2 * j + 1], c)

    _, l, acc = lax.fori_loop(0, nt // 2, pair, carry, unroll=2 if nt % 4 == 0 else 1)
    o_ref[...] = jnp.transpose(acc / l).astype(BF16)


def _attention(q_t, k_l, v_l, k_c, v_c, tq):
    b, nt, h, _, tn = q_t.shape
    n = nt * tn
    per = tn // tq
    tk = k_l.shape[3]
    tkc = k_c.shape[3]
    assert nt % 2 == 0, "key chunks are processed in pairs"
    return pl.pallas_call(
        _attn_kernel,
        grid=(b, h, n // tq),
        in_specs=[pl.BlockSpec((None, None, None, QK_DIM, tq), lambda bi, hi, qi: (bi, qi // per, hi, 0, qi % per)),
                  pl.BlockSpec((None, nt, None, tk, QK_DIM), lambda bi, hi, qi: (bi, 0, hi, 0, 0)),
                  pl.BlockSpec((None, 1, None, tkc, QK_DIM), lambda bi, hi, qi: (bi, 0, hi, 0, 0)),
                  pl.BlockSpec((None, nt, None, V_HEAD, tk), lambda bi, hi, qi: (bi, 0, hi, 0, 0)),
                  pl.BlockSpec((None, 1, None, V_HEAD, tkc), lambda bi, hi, qi: (bi, 0, hi, 0, 0))],
        out_specs=pl.BlockSpec((None, tq, V_HEAD), lambda bi, hi, qi: (bi, qi, hi)),
        out_shape=jax.ShapeDtypeStruct((b, n, h * V_HEAD), BF16),
        scratch_shapes=[pltpu.VMEM((tk, tq), F32), pltpu.VMEM((tk, tq), F32)],
        compiler_params=_cparams(("parallel", "parallel", "arbitrary")),
        name="attention",
    )(q_t, k_l, k_c, v_l, v_c)


_HALO = 16


_XBC_WIDTH = SSD_INNER + 2 * SSD_GROUPS * SSD_STATE
_CONV_COLS = 512


def _conv_kernel(p_ref, x_ref, n_ref, w_ref, b_ref, xg_ref, bg_ref, cg_ref, buf):
    i = pl.program_id(1)
    t = x_ref.shape[0]
    buf[0:_HALO, :] = jnp.where(i > 0, p_ref[...].astype(F32), 0.0)
    buf[_HALO:_HALO + t, :] = x_ref[...].astype(F32)
    buf[_HALO + t:, :] = jnp.where(i < pl.num_programs(1) - 1, n_ref[...].astype(F32), 0.0)
    for cb in range(_XBC_WIDTH // _CONV_COLS):
        cols = slice(cb * _CONV_COLS, (cb + 1) * _CONV_COLS)
        acc = b_ref[:, cols] + w_ref[0:1, cols] * buf[_HALO - 2:_HALO - 2 + t, cols]
        for k in range(1, SSD_CONV):
            acc = acc + w_ref[k:k + 1, cols] * buf[_HALO - 2 + k:_HALO - 2 + k + t, cols]
        y = _silu(acc).astype(BF16)
        if cb < SSD_GROUPS:
            xg_ref[cb] = y
        else:
            for q in range(_CONV_COLS // SSD_STATE):
                g = (cb - SSD_GROUPS) * (_CONV_COLS // SSD_STATE) + q
                piece = y[:, q * SSD_STATE:(q + 1) * SSD_STATE]
                if g < SSD_GROUPS:
                    bg_ref[g] = piece
                else:
                    cg_ref[g - SSD_GROUPS] = piece


def _conv(u_main, conv_w, conv_b, tc):
    b, n, _ = u_main.shape
    nh = n // _HALO
    per = tc // _HALO
    c0 = U_XBC // _XBC_WIDTH
    assert c0 * _XBC_WIDTH == U_XBC
    g = SSD_GROUPS
    return pl.pallas_call(
        _conv_kernel,
        grid=(b, n // tc),
        in_specs=[pl.BlockSpec((None, _HALO, _XBC_WIDTH), lambda bi, i: (bi, jnp.maximum(i * per - 1, 0), c0)),
                  pl.BlockSpec((None, tc, _XBC_WIDTH), lambda bi, i: (bi, i, c0)),
                  pl.BlockSpec((None, _HALO, _XBC_WIDTH), lambda bi, i: (bi, jnp.minimum((i + 1) * per, nh - 1), c0)),
                  pl.BlockSpec((SSD_CONV, _XBC_WIDTH), lambda bi, i: (0, 0)),
                  pl.BlockSpec((1, _XBC_WIDTH), lambda bi, i: (0, 0))],
        out_specs=[pl.BlockSpec((None, g, tc, GROUP_WIDTH), lambda bi, i: (bi, 0, i, 0)),
                   pl.BlockSpec((None, g, tc, SSD_STATE), lambda bi, i: (bi, 0, i, 0)),
                   pl.BlockSpec((None, g, tc, SSD_STATE), lambda bi, i: (bi, 0, i, 0))],
        out_shape=[jax.ShapeDtypeStruct((b, g, n, GROUP_WIDTH), BF16),
                   jax.ShapeDtypeStruct((b, g, n, SSD_STATE), BF16),
                   jax.ShapeDtypeStruct((b, g, n, SSD_STATE), BF16)],
        scratch_shapes=[pltpu.VMEM((tc + 2 * _HALO, _XBC_WIDTH), F32)],
        compiler_params=_cparams(("parallel", "arbitrary")),
        name="conv",
    )(u_main, u_main, u_main, conv_w, conv_b.reshape(1, -1))


def _dt_kernel(dt_ref, bias_ref, a_ref, dtb_ref, dob_ref, dsb_ref, cst_ref, dc_ref):
    q = dt_ref.shape[0]
    dt = jax.nn.softplus(dt_ref[...] + bias_ref[...])
    a = dt * a_ref[...]
    row = lax.broadcasted_iota(jnp.int32, a.shape, 0)
    cs = a
    s = 1
    while s < q:
        cs = cs + jnp.where(row >= s, pltpu.roll(cs, s, 0), 0.0)
        s *= 2
    total = cs[q - 1:q, :]
    lane = lax.broadcasted_iota(jnp.int32, a.shape, 1)
    cs = jnp.where(lane < SSD_HEADS, cs, total - cs + a)
    dtb_ref[...] = dt.astype(BF16)
    dob_ref[...] = jnp.exp(cs).astype(BF16)
    dsb_ref[...] = jnp.exp(total - cs).astype(BF16)
    cst_ref[...] = jnp.transpose(cs)
    dc_ref[...] = jnp.broadcast_to(jnp.exp(total), dc_ref.shape)


def _dt_prep(u_small, dt_bias, a_coef):
    b, n, _ = u_small.shape
    q = SSD_CHUNK
    nc = n // q
    vec = lambda: pl.BlockSpec((1, 128), lambda bi, c: (0, 0))
    tok = lambda: pl.BlockSpec((None, q, 128), lambda bi, c: (bi, c, 0))
    return pl.pallas_call(
        _dt_kernel,
        grid=(b, nc),
        in_specs=[tok(), vec(), vec()],
        out_specs=[tok(), tok(), tok(),
                   pl.BlockSpec((None, None, 128, q), lambda bi, c: (bi, c, 0, 0)),
                   pl.BlockSpec((None, None, 8, 128), lambda bi, c: (bi, c, 0, 0))],
        out_shape=[jax.ShapeDtypeStruct((b, n, 128), BF16)] * 3
        + [jax.ShapeDtypeStruct((b, nc, 128, q), F32), jax.ShapeDtypeStruct((b, nc, 8, 128), F32)],
        compiler_params=_cparams(("parallel", "parallel")),
        name="dt_prep",
    )(u_small, dt_bias, a_coef)


_TN = (((0,), (0,)), ((), ()))


def _ssd_kernel(x_ref, b_ref, c_ref, dt_ref, do_ref, ds_ref, cst_ref, dc_ref, e_ref, init_ref,
                y_ref, fin_ref, st_ref, *, direction, cpb):
    q = SSD_CHUNK
    step = pl.program_id(1)

    @pl.when(step == 0)
    def _():
        st_ref[...] = init_ref[...]

    ri = lax.broadcasted_iota(jnp.int32, (q, q), 0)
    ci = lax.broadcasted_iota(jnp.int32, (q, q), 1)
    mask = (ri >= ci) if direction == 0 else (ri <= ci)
    lane = lax.broadcasted_iota(jnp.int32, (q, 2 * SSD_HEADDIM), 1)
    left = lane < SSD_HEADDIM

    def chunk_body(cc, carry):
        c = cc if direction == 0 else cpb - 1 - cc
        rows = pl.ds(pl.multiple_of(c * q, q), q)
        dtb = dt_ref[rows, :]
        dob = do_ref[rows, :]
        dsb = ds_ref[rows, :]

        def group_body(g, carry2):
            eg = e_ref[g]
            dtx = jnp.dot(dtb, eg, preferred_element_type=F32)
            dox = jnp.dot(dob, eg, preferred_element_type=F32)
            dsx = jnp.dot(dsb, eg, preferred_element_type=F32)
            xdt = x_ref[g, rows, :].astype(F32) * dtx
            xdt_b = xdt.astype(BF16)
            xds_b = (xdt * dsx).astype(BF16)
            bg = b_ref[g, rows, :]
            cg = c_ref[g, rows, :]
            cb = lax.dot_general(cg, bg, _NT, preferred_element_type=F32)
            st = st_ref[g]
            y_off = jnp.dot(cg, st.astype(BF16), preferred_element_type=F32) * dox
            g8 = cst_ref[c, pl.ds(pl.multiple_of(direction * SSD_HEADS + g * HEADS_PER_GROUP, 8), 8), :]
            g8t = jnp.transpose(g8)
            pieces = []
            for pr in range(HEADS_PER_GROUP // 2):
                pair = xdt_b[:, pr * 128:(pr + 1) * 128]
                acc = None
                for half in range(2):
                    r = 2 * pr + half
                    seg = g8t[:, r:r + 1] - g8[r:r + 1, :]
                    m = (cb * jnp.exp(jnp.where(mask, seg, -jnp.inf))).astype(BF16)
                    xin = jnp.where(left if half == 0 else jnp.logical_not(left), pair, jnp.zeros_like(pair))
                    d = jnp.dot(m, xin, preferred_element_type=F32)
                    acc = d if acc is None else acc + d
                pieces.append(acc)
            y = jnp.concatenate(pieces, axis=1) + y_off
            y_ref[g, rows, :] = y.astype(BF16)
            st_new = lax.dot_general(bg, xds_b, _TN, preferred_element_type=F32)
            st_ref[g] = st * dc_ref[c, pl.ds(g, 1), :] + st_new
            return carry2

        lax.fori_loop(0, SSD_GROUPS, group_body, 0)
        return carry

    lax.fori_loop(0, cpb, chunk_body, 0)

    @pl.when(step == pl.num_programs(1) - 1)
    def _():
        fin_ref[...] = st_ref[...]


def _ssd(xg, bg, cg, dtb, dob, dsb, cst, dcx, e_mat, init, direction, cpb):
    b, g, n, _ = xg.shape
    q = SSD_CHUNK
    t = q * cpb
    nt = n // t
    ti = (lambda i: i) if direction == 0 else (lambda i: nt - 1 - i)
    grp = lambda w: pl.BlockSpec((None, g, t, w), lambda bi, i: (bi, 0, ti(i), 0))
    tok = lambda: pl.BlockSpec((None, t, 128), lambda bi, i: (bi, ti(i), 0))
    return pl.pallas_call(
        functools.partial(_ssd_kernel, direction=direction, cpb=cpb),
        grid=(b, nt),
        in_specs=[grp(GROUP_WIDTH), grp(SSD_STATE), grp(SSD_STATE), tok(), tok(), tok(),
                  pl.BlockSpec((None, cpb, 128, q), lambda bi, i: (bi, ti(i), 0, 0)),
                  pl.BlockSpec((None, cpb, None, g, GROUP_WIDTH), lambda bi, i: (bi, ti(i), direction, 0, 0)),
                  pl.BlockSpec((None, g, 128, GROUP_WIDTH), lambda bi, i: (direction, 0, 0, 0)),
                  pl.BlockSpec((None, g, SSD_STATE, GROUP_WIDTH), lambda bi, i: (bi, 0, 0, 0))],
        out_specs=[grp(GROUP_WIDTH),
                   pl.BlockSpec((None, g, SSD_STATE, GROUP_WIDTH), lambda bi, i: (bi, 0, 0, 0))],
        out_shape=[jax.ShapeDtypeStruct((b, g, n, GROUP_WIDTH), BF16),
                   jax.ShapeDtypeStruct((b, g, SSD_STATE, GROUP_WIDTH), F32)],
        scratch_shapes=[pltpu.VMEM((g, SSD_STATE, GROUP_WIDTH), F32)],
        compiler_params=_cparams(("parallel", "arbitrary")),
        name="ssd_fwd" if direction == 0 else "ssd_bwd",
    )(xg, bg, cg, dtb, dob, dsb, cst, dcx, e_mat, init)


def _ssd_out_kernel(x_ref, yf_ref, yb_ref, z0_ref, z1_ref, d_ref, g_ref, w_ref, o_ref):
    ys = []
    ss = None
    half = SSD_GROUPS // 2
    for g in range(SSD_GROUPS):
        cols = slice(g * GROUP_WIDTH, (g + 1) * GROUP_WIDTH)
        z_ref = z0_ref if g < half else z1_ref
        zcols = slice((g % half) * GROUP_WIDTH, (g % half + 1) * GROUP_WIDTH)
        y = d_ref[:, cols] * x_ref[g].astype(F32) + yf_ref[g].astype(F32) + yb_ref[g].astype(F32)
        y = y * _silu(z_ref[:, zcols].astype(F32))
        sq = jnp.sum(y * y, axis=-1, keepdims=True)
        ss = sq if ss is None else ss + sq
        ys.append(y)
    inv = lax.rsqrt(ss * (1.0 / SSD_INNER) + NORM_EPS)
    acc = None
    for g in range(SSD_GROUPS):
        cols = slice(g * GROUP_WIDTH, (g + 1) * GROUP_WIDTH)
        yn = (ys[g] * inv * g_ref[:, cols]).astype(BF16)
        d = jnp.dot(yn, w_ref[cols, :], preferred_element_type=F32)
        acc = d if acc is None else acc + d
    o_ref[...] = acc.astype(BF16)


def _ssd_out(xg, yf, yb, u_main, d_exp, g_ssd, w, tm):
    b, g, n, gw = xg.shape
    dm = w.shape[1]
    grp = lambda: pl.BlockSpec((None, g, tm, gw), lambda bi, i: (bi, 0, i, 0))
    return pl.pallas_call(
        _ssd_out_kernel,
        grid=(b, n // tm),
        in_specs=[grp(), grp(), grp(),
                  pl.BlockSpec((None, tm, HALF_INNER), lambda bi, i: (bi, i, U_Z // HALF_INNER)),
                  pl.BlockSpec((None, tm, HALF_INNER), lambda bi, i: (bi, i, U_Z // HALF_INNER + 1)),
                  pl.BlockSpec((1, SSD_INNER), lambda bi, i: (0, 0)),
                  pl.BlockSpec((1, SSD_INNER), lambda bi, i: (0, 0)),
                  pl.BlockSpec((SSD_INNER, dm), lambda bi, i: (0, 0))],
        out_specs=pl.BlockSpec((None, tm, dm), lambda bi, i: (bi, i, 0)),
        out_shape=jax.ShapeDtypeStruct((b, n, dm), BF16),
        compiler_params=_cparams(("parallel", "parallel")),
        name="ssd_out",
    )(xg, yf, yb, u_main, u_main, d_exp, g_ssd, w)


def _pack_pairs(x):
    k = x.shape[1] // 2
    u = lax.bitcast_convert_type(x.astype(BF16).astype(F32), U32)
    return (u[:, k:] & jnp.uint32(0xFFFF0000)) | (u[:, :k] >> 16)


def _unpack_pairs(u):
    lo = lax.bitcast_convert_type(u << 16, F32)
    hi = lax.bitcast_convert_type(u & jnp.uint32(0xFFFF0000), F32)
    return jnp.concatenate([lo, hi], axis=1)


_ROUTER_LANES = 128


def _route(lg):
    lane = lax.broadcasted_iota(jnp.int32, lg.shape, 1).astype(F32)
    big = float(_ROUTER_LANES)

    def first_max(v):
        mx = jnp.max(v, axis=-1, keepdims=True)
        return mx, jnp.min(jnp.where(v == mx, lane, big), axis=-1, keepdims=True)

    gl = jnp.where(lane < N_GROUPS_E, lg, -jnp.inf)
    gmax, g_sel = first_max(gl)
    g_top = 1.0 / jnp.sum(jnp.exp(gl - gmax), axis=-1, keepdims=True)
    lo = N_GROUPS_E + EXPERTS_PER_GROUP * g_sel
    el = jnp.where((lane >= lo) & (lane < lo + EXPERTS_PER_GROUP), lg, -jnp.inf)
    e1, i1 = first_max(el)
    e2, i2 = first_max(jnp.where(lane == i1, -jnp.inf, el))
    r = jnp.exp(e2 - e1)
    w1 = g_top / (1.0 + r)
    w2 = w1 * r
    ids = jnp.where(lane == 0.0, i1 - N_GROUPS_E, jnp.where(lane == 1.0, i2 - N_GROUPS_E, 0.0)).astype(jnp.int32)
    wts = jnp.where(lane == 0.0, w1, jnp.where(lane == 1.0, w2, 0.0))
    return ids, wts


def _merge_kernel(s_ref, o_ref, gs_ref, gm_ref, x_ref, ga_ref, shf_ref, scf_ref, gpost_ref, gpre_ref,
                  wo_ref, wout_ref, wrh_ref, wrl_ref, br_ref, x1_ref, hp_ref, id_ref, wt_ref):
    y_mla = jnp.dot(o_ref[...], wo_ref[...], preferred_element_type=F32)
    mix = (jax.nn.sigmoid(gs_ref[...].astype(F32)) * s_ref[...].astype(F32)
           + jax.nn.sigmoid(gm_ref[...].astype(F32)) * y_mla)
    ym = jnp.dot(mix.astype(BF16), wout_ref[...], preferred_element_type=F32)
    x1 = x_ref[...] + ga_ref[...] * (_rms(ym) * gpost_ref[...])
    x1_ref[...] = x1
    h2 = _rms(x1) * (gpre_ref[...] * (1.0 + scf_ref[...])) + shf_ref[...]
    hp_ref[...] = _pack_pairs(h2)
    hh = h2.astype(BF16)
    hl = (h2 - hh.astype(F32)).astype(BF16)
    lg = (jnp.dot(hh, wrh_ref[...], preferred_element_type=F32)
          + jnp.dot(hh, wrl_ref[...], preferred_element_type=F32)
          + jnp.dot(hl, wrh_ref[...], preferred_element_type=F32))
    ids, wts = _route(lg + br_ref[...])
    id_ref[...] = ids
    wt_ref[...] = wts


def _merge(s, o, u_main, x, g_a, sh_f, sc_f, g_post, g_pre, w_o, w_out, wr_hi, wr_lo, b_r, tm):
    b, n, d = x.shape
    tokd = lambda: pl.BlockSpec((None, tm, d), lambda bi, i: (bi, i, 0))
    modv = lambda: pl.BlockSpec((None, 1, d), lambda bi, i: (bi, 0, 0))
    vec = lambda w: pl.BlockSpec((1, w), lambda bi, i: (0, 0))
    mat = lambda r, c: pl.BlockSpec((r, c), lambda bi, i: (0, 0))
    return pl.pallas_call(
        _merge_kernel,
        grid=(b, n // tm),
        in_specs=[tokd(), tokd(),
                  pl.BlockSpec((None, tm, d), lambda bi, i: (bi, i, U_GATE // d)),
                  pl.BlockSpec((None, tm, d), lambda bi, i: (bi, i, U_GATE // d + 1)),
                  tokd(), modv(), modv(), modv(), vec(d), vec(d),
                  mat(d, d), mat(d, d), mat(d, _ROUTER_LANES), mat(d, _ROUTER_LANES), vec(_ROUTER_LANES)],
        out_specs=[tokd(),
                   pl.BlockSpec((None, tm, d // 2), lambda bi, i: (bi, i, 0)),
                   pl.BlockSpec((None, tm, _ROUTER_LANES), lambda bi, i: (bi, i, 0)),
                   pl.BlockSpec((None, tm, _ROUTER_LANES), lambda bi, i: (bi, i, 0))],
        out_shape=[jax.ShapeDtypeStruct((b, n, d), F32),
                   jax.ShapeDtypeStruct((b, n, d // 2), U32),
                   jax.ShapeDtypeStruct((b, n, _ROUTER_LANES), jnp.int32),
                   jax.ShapeDtypeStruct((b, n, _ROUTER_LANES), F32)],
        compiler_params=_cparams(("parallel", "parallel")),
        name="merge",
    )(s, o, u_main, u_main, x, g_a, sh_f, sc_f, g_post, g_pre, w_o, w_out, wr_hi, wr_lo, b_r)


_GATHER_UNROLL = 8


def _row_copy(src_ref, dst_ref, sem, src_row, dst_row):
    return pltpu.make_async_copy(src_ref.at[pl.ds(src_row, 1), :], dst_ref.at[pl.ds(dst_row, 1), :], sem)


def _gather_start(idx_ref, base, src_ref, dst_ref, sem):
    def issue(r, carry):
        _row_copy(src_ref, dst_ref, sem, idx_ref[base + r], r).start()
        return carry

    lax.fori_loop(0, dst_ref.shape[0], issue, 0, unroll=_GATHER_UNROLL)


def _gather_wait(src_ref, dst_ref, sem):
    def drain(r, carry):
        _row_copy(src_ref, dst_ref, sem, 0, r).wait()
        return carry

    lax.fori_loop(0, dst_ref.shape[0], drain, 0, unroll=_GATHER_UNROLL)


def _expert_kernel(be_ref, nu_ref, src_ref, tid_ref, h_ref, wg_ref, wu_ref, wd_ref, y_ref, x_buf, sem):
    i = pl.program_id(0)
    n_used = nu_ref[0]
    slot = i % 2

    def start_block(blk, dst_slot):
        base = src_ref[blk]
        for r in range(MOE_BLOCK):
            _row_copy(h_ref, x_buf.at[dst_slot], sem.at[dst_slot], tid_ref[base + r], r).start()

    def wait_block(dst_slot):
        for r in range(MOE_BLOCK):
            _row_copy(h_ref, x_buf.at[dst_slot], sem.at[dst_slot], 0, r).wait()

    @pl.when(i == 0)
    def _():
        start_block(0, 0)

    @pl.when(i < n_used)
    def _():
        wait_block(slot)
        start_block(i + 1, 1 - slot)
        x = _unpack_pairs(x_buf[slot]).astype(BF16)
        hg = jnp.dot(x, wg_ref[...], preferred_element_type=F32)
        hu = jnp.dot(x, wu_ref[...], preferred_element_type=F32)
        a = (_silu(hg) * hu).astype(BF16)
        y_ref[...] = _pack_pairs(jnp.dot(a, wd_ref[...], preferred_element_type=F32))

    @pl.when(i == n_used)
    def _():
        wait_block(slot)

    @pl.when(i >= n_used)
    def _():
        y_ref[...] = jnp.zeros(y_ref.shape, y_ref.dtype)


def _experts(h_pack, tid_sorted, blk_e, blk_src, n_used, w_gate, w_up, w_down):
    nb = blk_e.shape[0]
    dh = h_pack.shape[1]
    ne, d, ff = w_gate.shape
    wspec = lambda r, c: pl.BlockSpec((None, r, c), lambda i, be, nu, src, tid: (be[i], 0, 0))
    return pl.pallas_call(
        _expert_kernel,
        grid_spec=pltpu.PrefetchScalarGridSpec(
            num_scalar_prefetch=4,
            grid=(nb,),
            in_specs=[pl.BlockSpec(memory_space=pl.ANY), wspec(d, ff), wspec(d, ff), wspec(ff, d)],
            out_specs=pl.BlockSpec((MOE_BLOCK, dh), lambda i, be, nu, src, tid: (i, 0)),
            scratch_shapes=[pltpu.VMEM((2, MOE_BLOCK, dh), U32), pltpu.SemaphoreType.DMA((2,))]),
        out_shape=jax.ShapeDtypeStruct((nb * MOE_BLOCK, dh), U32),
        compiler_params=_cparams(("arbitrary",)),
        name="experts",
    )(blk_e, n_used, blk_src, tid_sorted, h_pack, w_gate, w_up, w_down)


def _final_kernel(p0_ref, p1_ref, y_ref, w_ref, x1_ref, gf_ref, gpost_ref, o_ref, y0_buf, y1_buf, sem0, sem1):
    tm = y0_buf.shape[0]
    base = (pl.program_id(0) * pl.num_programs(1) + pl.program_id(1)) * tm
    _gather_start(p0_ref, base, y_ref, y0_buf, sem0)
    _gather_start(p1_ref, base, y_ref, y1_buf, sem1)
    _gather_wait(y_ref, y0_buf, sem0)
    _gather_wait(y_ref, y1_buf, sem1)
    w = w_ref[...]
    f = w[:, 0:1] * _unpack_pairs(y0_buf[...]) + w[:, 1:2] * _unpack_pairs(y1_buf[...])
    o_ref[...] = x1_ref[...] + gf_ref[...] * (_rms(f) * gpost_ref[...])


def _final(y_buf, pos0, pos1, wts, x1, g_f, g_post, tm):
    b, n, d = x1.shape
    return pl.pallas_call(
        _final_kernel,
        grid_spec=pltpu.PrefetchScalarGridSpec(
            num_scalar_prefetch=2,
            grid=(b, n // tm),
            in_specs=[pl.BlockSpec(memory_space=pl.ANY),
                      pl.BlockSpec((None, tm, 128), lambda bi, i, p0, p1: (bi, i, 0)),
                      pl.BlockSpec((None, tm, d), lambda bi, i, p0, p1: (bi, i, 0)),
                      pl.BlockSpec((None, 1, d), lambda bi, i, p0, p1: (bi, 0, 0)),
                      pl.BlockSpec((1, d), lambda bi, i, p0, p1: (0, 0))],
            out_specs=pl.BlockSpec((None, tm, d), lambda bi, i, p0, p1: (bi, i, 0)),
            scratch_shapes=[pltpu.VMEM((tm, d // 2), U32), pltpu.VMEM((tm, d // 2), U32),
                            pltpu.SemaphoreType.DMA(()), pltpu.SemaphoreType.DMA(())]),
        out_shape=jax.ShapeDtypeStruct((b, n, d), F32),
        compiler_params=_cparams(("arbitrary", "arbitrary")),
        name="final",
    )(pos0, pos1, y_buf, wts, x1, g_f, g_post)


def _lookup(table, idx):
    e = jnp.arange(table.shape[0], dtype=jnp.int32)
    return jnp.sum(jnp.where(idx[:, None] == e[None, :], table[None, :], 0), axis=1)


def _dispatch_plan(expert_id):
    t = expert_id.shape[0]
    n_assign = t * TOP_K
    eid = expert_id.reshape(n_assign)
    eid_s, order = lax.sort((eid, jnp.arange(n_assign, dtype=jnp.int32)), num_keys=1)
    experts = jnp.arange(N_EXPERTS, dtype=jnp.int32)
    start = jnp.sum((eid_s[None, :] < experts[:, None]).astype(jnp.int32), axis=1)
    counts = jnp.concatenate([start[1:], jnp.full((1,), n_assign, jnp.int32)]) - start
    nblk_e = (counts + MOE_BLOCK - 1) // MOE_BLOCK
    bend = jnp.cumsum(nblk_e)
    pstart = (bend - nblk_e) * MOE_BLOCK
    n_used = bend[-1].astype(jnp.int32)
    n_blocks = n_assign // MOE_BLOCK + N_EXPERTS + 1
    blk = jnp.arange(n_blocks, dtype=jnp.int32)
    blk_e = jnp.sum((bend[None, :] <= jnp.minimum(blk, n_used - 1)[:, None]).astype(jnp.int32), axis=1)
    blk_e = jnp.clip(blk_e, 0, N_EXPERTS - 1)
    blk_src = _lookup(start - pstart, blk_e) + blk * MOE_BLOCK
    blk_src = jnp.where(blk < n_used, blk_src, 0).astype(jnp.int32)
    dest_s = jnp.arange(n_assign, dtype=jnp.int32) + _lookup(pstart - start, eid_s)
    _, pos = lax.sort((order, dest_s), num_keys=1)
    tid_sorted = jnp.concatenate([order // TOP_K, jnp.zeros((MOE_BLOCK,), jnp.int32)])
    return pos, tid_sorted, blk_e.astype(jnp.int32), blk_src, n_used.reshape(1)


def _rope_tables(n):
    rows = n // GRID_W
    row = jnp.repeat(jnp.arange(rows), GRID_W).astype(F32)
    col = jnp.tile(jnp.arange(GRID_W), rows).astype(F32)
    half = QK_ROPE // 2
    freqs = ROPE_THETA ** (-jnp.arange(0, half, 2, dtype=F32) / half)
    ar = row[:, None] * freqs
    ac = col[:, None] * freqs
    cos = jnp.concatenate([jnp.cos(ar), jnp.cos(ar), jnp.cos(ac), jnp.cos(ac)], axis=1)
    sin = jnp.concatenate([-jnp.sin(ar), jnp.sin(ar), -jnp.sin(ac), jnp.sin(ac)], axis=1)
    return cos.T, sin.T, cos, sin


def _pick(n, pref):
    return pref if n % pref == 0 else n


def kernel(x, c, ctx, c_ctx, w_ada, b_ada, g_pre_mix, g_post_mix, g_pre_ffn, g_post_ffn, w_in, g_q, w_uq,
           g_kv, w_ukv, w_o_mla, conv_w, conv_b, a_log, dt_bias, d_skip, g_ssd, w_ssd_out, w_out,
           w_router_group, b_router_group, w_router_expert, b_router_expert, w_exp_gate, w_exp_up, w_exp_down):
    assert w_ada.shape[0] == 1, "single-layer block"
    b, n, d = x.shape
    n_ctx = ctx.shape[1]
    h = MLA_HEADS

    w_in0 = w_in[0]
    o_ckv = Q_LORA
    o_kr = o_ckv + KV_LORA
    o_z = o_kr + QK_ROPE
    o_xbc = o_z + SSD_INNER
    o_dt = o_xbc + SSD_INNER + 2 * SSD_GROUPS * SSD_STATE
    o_gate = o_dt + 2 * SSD_HEADS
    w_kr = w_in0[:, o_kr:o_z]
    swap = jnp.concatenate([jnp.arange(16, 32), jnp.arange(0, 16), jnp.arange(48, 64), jnp.arange(32, 48)])
    assert d == HALF_INNER
    w_main = jnp.concatenate([w_in0[:, o_xbc:o_dt], w_in0[:, o_z:o_xbc], w_in0[:, o_gate:],
                              w_in0[:, :o_ckv], w_in0[:, o_ckv:o_kr]], axis=1).astype(BF16)
    w_small = jnp.concatenate([w_in0[:, o_dt:o_gate], w_kr, w_kr[:, swap]], axis=1).astype(BF16)
    wq_t = w_uq[0].reshape(Q_LORA, h, QK_DIM).transpose(1, 2, 0).astype(BF16)
    wkv = w_ukv[0].reshape(KV_LORA, h, QK_NOPE + V_HEAD)
    wk = wkv[:, :, :QK_NOPE].transpose(1, 0, 2).astype(BF16)
    wv_t = wkv[:, :, QK_NOPE:].transpose(1, 2, 0).astype(BF16)
    w_o = w_o_mla[0].astype(BF16)
    w_so = w_ssd_out[0].astype(BF16)
    w_mix = w_out[0].astype(BF16)
    w_r = jnp.concatenate([w_router_group[0], w_router_expert[0],
                           jnp.zeros((d, 128 - N_GROUPS_E - N_EXPERTS), F32)], axis=1)
    wr_hi = w_r.astype(BF16)
    wr_lo = (w_r - wr_hi.astype(F32)).astype(BF16)
    b_r = jnp.concatenate([b_router_group[0], b_router_expert[0],
                           jnp.zeros((128 - N_GROUPS_E - N_EXPERTS,), F32)]).reshape(1, 128)
    wg = w_exp_gate[0].astype(BF16)
    wu = w_exp_up[0].astype(BF16)
    wd = w_exp_down[0].astype(BF16)
    vec = lambda v: v.reshape(1, -1)

    c8 = jnp.concatenate([c, c_ctx[None, :], jnp.zeros((8 - b - 1, d), F32)], axis=0)
    mod = _ada(c8, w_ada[0], b_ada[0])
    sh_a, sc_a, g_a, sh_f, sc_f, g_f = [mod[:, i * d:(i + 1) * d] for i in range(N_MOD)]
    lat = lambda m: m[:b].reshape(b, 1, d)
    cmod = lambda m: jnp.broadcast_to(m[b:b + 1].reshape(1, 1, d), (b, 1, d))

    u_main, u_small = _inproj(x, lat(sc_a), lat(sh_a), vec(g_pre_mix[0]), w_main, w_small, _pick(n, 512))
    uc_main, uc_small = _inproj(ctx, cmod(sc_a), cmod(sh_a), vec(g_pre_mix[0]), w_main, w_small, n_ctx)

    tn = _pick(n, 512)
    q_t, k_l, v_l = _qkv(u_main, u_small, vec(g_q[0]), vec(g_kv[0]), wq_t, wk, wv_t, _rope_tables(n), tn)
    k_c, v_c = _qkv(uc_main, uc_small, None, vec(g_kv[0]), None, wk, wv_t, None, n_ctx)
    o_mla = _attention(q_t, k_l, v_l, k_c, v_c, tn)

    a_coef = -jnp.exp(a_log[0].astype(F32)).reshape(1, 2 * SSD_HEADS)
    dtb_flat = dt_bias[0].astype(F32).reshape(1, 2 * SSD_HEADS)
    hh = jnp.arange(2 * SSD_HEADS)
    lane_head = jnp.arange(GROUP_WIDTH) // SSD_HEADDIM
    e_mat = (hh[None, None, :, None] == (jnp.arange(2)[:, None, None, None] * SSD_HEADS
                                        + jnp.arange(SSD_GROUPS)[None, :, None, None] * HEADS_PER_GROUP
                                        + lane_head[None, None, None, :])).astype(BF16)

    def ssd_inputs(um, us, tc):
        xg, bg, cg = _conv(um, conv_w[0], conv_b[0], tc)
        dtb, dob, dsb, cst, dc = _dt_prep(us, dtb_flat, a_coef)
        nc = dc.shape[1]
        dcx = jnp.repeat(dc[:, :, 0, :].reshape(b, nc, 2, SSD_GROUPS, HEADS_PER_GROUP), SSD_HEADDIM, axis=-1)
        return xg, bg, cg, dtb, dob, dsb, cst, dcx

    zero_state = jnp.zeros((b, SSD_GROUPS, SSD_STATE, GROUP_WIDTH), F32)
    ctx_in = ssd_inputs(uc_main, uc_small, n_ctx)
    lat_in = ssd_inputs(u_main, u_small, _pick(n, 256))
    cpb_c = n_ctx // SSD_CHUNK
    cpb = _pick(n, 512) // SSD_CHUNK
    ys = []
    for direction in range(2):
        _, s0 = _ssd(*ctx_in, e_mat, zero_state, direction, cpb_c)
        y_dir, _ = _ssd(*lat_in, e_mat, s0, direction, cpb)
        ys.append(y_dir)
    d_exp = jnp.repeat(d_skip[0].astype(F32), SSD_HEADDIM).reshape(1, SSD_INNER)
    s_l = _ssd_out(lat_in[0], ys[0], ys[1], u_main, d_exp, vec(g_ssd[0]), w_so, _pick(n, 256))

    x1, h_pack, ids, wts = _merge(s_l, o_mla, u_main, x, lat(g_a), lat(sh_f), lat(sc_f), vec(g_post_mix[0]),
                                  vec(g_pre_ffn[0]), w_o, w_mix, wr_hi, wr_lo, b_r, _pick(n, 256))

    t = b * n
    pos, tid_sorted, blk_e, blk_src, n_used = _dispatch_plan(ids.reshape(t, _ROUTER_LANES)[:, :TOP_K])
    y_buf = _experts(h_pack.reshape(t, d // 2), tid_sorted, blk_e, blk_src, n_used, wg, wu, wd)
    pos = pos.reshape(t, TOP_K)
    return _final(y_buf, pos[:, 0], pos[:, 1], wts, x1, lat(g_f), vec(g_post_ffn[0]), _pick(n, 256))
```

```python
import functools
import math

import jax
import jax.numpy as jnp
from jax import lax
from jax.experimental import pallas as pl
from jax.experimental.pallas import tpu as pltpu

F32 = jnp.float32
BF16 = jnp.bfloat16
U32 = jnp.uint32

GRID_W = 64
N_MOD = 6
NORM_EPS = 1e-6
MLA_HEADS = 16
Q_LORA = 512
KV_LORA = 512
QK_NOPE = 128
QK_ROPE = 64
QK_DIM = QK_NOPE + QK_ROPE
V_HEAD = 128
ROPE_THETA = 10000.0
ATTN_SCALE = QK_DIM ** -0.5
SSD_HEADDIM = 64
SSD_HEADS = 64
SSD_INNER = SSD_HEADS * SSD_HEADDIM
SSD_GROUPS = 8
SSD_STATE = 128
SSD_CONV = 5
HEADS_PER_GROUP = SSD_HEADS // SSD_GROUPS
GROUP_WIDTH = HEADS_PER_GROUP * SSD_HEADDIM
N_GROUPS_E = 8
EXPERTS_PER_GROUP = 8
N_EXPERTS = N_GROUPS_E * EXPERTS_PER_GROUP
TOP_K = 2
LOG2E = 1.4426950408889634

U_XBC = 0
U_Z = U_XBC + SSD_INNER + 2 * SSD_GROUPS * SSD_STATE
U_GATE = U_Z + SSD_INNER
U_CQ = U_GATE + 2 * 2048
U_CKV = U_CQ + Q_LORA
U_MAIN = U_CKV + KV_LORA
HALF_INNER = SSD_INNER // 2
U_SMALL = 256

SSD_CHUNK = 128
MOE_BLOCK = 256
VMEM_LIMIT = 56 * 1024 * 1024


def _cparams(sem):
    return pltpu.CompilerParams(dimension_semantics=sem, vmem_limit_bytes=VMEM_LIMIT)


def _silu(x):
    return x * jax.nn.sigmoid(x)


def _rms(x, eps=NORM_EPS):
    return x * lax.rsqrt(jnp.mean(x * x, axis=-1, keepdims=True) + eps)


def _ada_kernel(c_ref, w_ref, b_ref, o_ref):
    c = c_ref[...]
    o_ref[...] = jnp.dot(_silu(c).astype(BF16), w_ref[...].astype(BF16),
                         preferred_element_type=F32) + b_ref[...]


def _ada(c8, w_ada, b_ada):
    d, n = w_ada.shape
    tn = 1536
    return pl.pallas_call(
        _ada_kernel,
        grid=(n // tn,),
        in_specs=[pl.BlockSpec((8, d), lambda j: (0, 0)),
                  pl.BlockSpec((d, tn), lambda j: (0, j)),
                  pl.BlockSpec((1, tn), lambda j: (0, j))],
        out_specs=pl.BlockSpec((8, tn), lambda j: (0, j)),
        out_shape=jax.ShapeDtypeStruct((8, n), F32),
        compiler_params=_cparams(("parallel",)),
        name="ada",
    )(c8, w_ada, b_ada.reshape(1, n))


def _inproj_kernel(x_ref, sc_ref, sh_ref, g_ref, wm_ref, ws_ref, om_ref, os_ref, h_scr):
    @pl.when(pl.program_id(2) == 0)
    def _():
        y = _rms(x_ref[...]) * (g_ref[...] * (1.0 + sc_ref[...])) + sh_ref[...]
        hb = y.astype(BF16)
        h_scr[...] = hb
        os_ref[...] = jnp.dot(hb, ws_ref[...], preferred_element_type=F32)

    om_ref[...] = jnp.dot(h_scr[...], wm_ref[...], preferred_element_type=F32).astype(BF16)


def _inproj(x, scale, shift, g, w_main, w_small, tm):
    b, n, d = x.shape
    tn = 1536
    return pl.pallas_call(
        _inproj_kernel,
        grid=(b, n // tm, U_MAIN // tn),
        in_specs=[pl.BlockSpec((None, tm, d), lambda bi, i, j: (bi, i, 0)),
                  pl.BlockSpec((None, 1, d), lambda bi, i, j: (bi, 0, 0)),
                  pl.BlockSpec((None, 1, d), lambda bi, i, j: (bi, 0, 0)),
                  pl.BlockSpec((1, d), lambda bi, i, j: (0, 0)),
                  pl.BlockSpec((d, tn), lambda bi, i, j: (0, j)),
                  pl.BlockSpec((d, U_SMALL), lambda bi, i, j: (0, 0))],
        out_specs=[pl.BlockSpec((None, tm, tn), lambda bi, i, j: (bi, i, j)),
                   pl.BlockSpec((None, tm, U_SMALL), lambda bi, i, j: (bi, i, 0))],
        out_shape=[jax.ShapeDtypeStruct((b, n, U_MAIN), BF16),
                   jax.ShapeDtypeStruct((b, n, U_SMALL), F32)],
        scratch_shapes=[pltpu.VMEM((tm, d), BF16)],
        compiler_params=_cparams(("parallel", "parallel", "arbitrary")),
        name="in_proj",
    )(x, scale, shift, g, w_main, w_small)


_NT = (((1,), (1,)), ((), ()))


def _qkv_kernel(*refs, rope, with_q):
    if rope:
        (cq_ref, ckv_ref, kr_ref, gq_ref, gkv_ref, wq_ref, wk_ref, wv_ref,
         cost_ref, sint_ref, cos_ref, sin_ref, q_ref, k_ref, v_ref) = refs
    else:
        ckv_ref, kr_ref, gkv_ref, wk_ref, wv_ref, k_ref, v_ref = refs
    ckvn = (_rms(ckv_ref[...].astype(F32)) * gkv_ref[...]).astype(BF16)
    kr = kr_ref[...]
    kpe = kr[:, :QK_ROPE]
    if rope:
        kpe = kpe * cos_ref[...] + kr[:, QK_ROPE:] * sin_ref[...]
    kpe = kpe.astype(BF16)
    if with_q:
        cqn = (_rms(cq_ref[...].astype(F32)) * gq_ref[...]).astype(BF16)
        cost = cost_ref[...]
        sint = sint_ref[...]

    def head(h, carry):
        if with_q:
            qt = lax.dot_general(wq_ref[h], cqn, _NT, preferred_element_type=F32)
            qp = qt[QK_NOPE:]
            qsw = jnp.concatenate([qp[16:32], qp[0:16], qp[48:64], qp[32:48]], axis=0)
            qp = qp * cost + qsw * sint
            q_ref[h, :QK_NOPE, :] = (qt[:QK_NOPE] * (ATTN_SCALE * LOG2E)).astype(BF16)
            q_ref[h, QK_NOPE:, :] = (qp * (ATTN_SCALE * LOG2E)).astype(BF16)
        kn = jnp.dot(ckvn, wk_ref[h], preferred_element_type=F32)
        k_ref[h, :, :QK_NOPE] = kn.astype(BF16)
        k_ref[h, :, QK_NOPE:] = kpe
        vt = lax.dot_general(wv_ref[h], ckvn, _NT, preferred_element_type=F32)
        v_ref[h] = vt.astype(BF16)
        return carry

    lax.fori_loop(0, MLA_HEADS, head, 0)


def _qkv(u_main, u_small, g_q, g_kv, wq_t, wk, wv_t, tables, tn):
    b, n, _ = u_main.shape
    nt = n // tn
    h = MLA_HEADS
    rope = tables is not None
    in_specs = []
    args = []
    if rope:
        in_specs.append(pl.BlockSpec((None, tn, Q_LORA), lambda bi, i: (bi, i, U_CQ // Q_LORA)))
        args.append(u_main)
    in_specs += [pl.BlockSpec((None, tn, KV_LORA), lambda bi, i: (bi, i, U_CKV // KV_LORA)),
                 pl.BlockSpec((None, tn, 128), lambda bi, i: (bi, i, 1))]
    args += [u_main, u_small]
    if rope:
        in_specs.append(pl.BlockSpec((1, Q_LORA), lambda bi, i: (0, 0)))
        args.append(g_q)
    in_specs.append(pl.BlockSpec((1, KV_LORA), lambda bi, i: (0, 0)))
    args.append(g_kv)
    if rope:
        in_specs.append(pl.BlockSpec((h, QK_DIM, Q_LORA), lambda bi, i: (0, 0, 0)))
        args.append(wq_t)
    in_specs += [pl.BlockSpec((h, KV_LORA, QK_NOPE), lambda bi, i: (0, 0, 0)),
                 pl.BlockSpec((h, V_HEAD, KV_LORA), lambda bi, i: (0, 0, 0))]
    args += [wk, wv_t]
    out_specs = []
    out_shape = []
    if rope:
        cos_t, sin_t, cos, sin = tables
        in_specs += [pl.BlockSpec((QK_ROPE, tn), lambda bi, i: (0, i)),
                     pl.BlockSpec((QK_ROPE, tn), lambda bi, i: (0, i)),
                     pl.BlockSpec((tn, QK_ROPE), lambda bi, i: (i, 0)),
                     pl.BlockSpec((tn, QK_ROPE), lambda bi, i: (i, 0))]
        args += [cos_t, sin_t, cos, sin]
        out_specs.append(pl.BlockSpec((None, None, h, QK_DIM, tn), lambda bi, i: (bi, i, 0, 0, 0)))
        out_shape.append(jax.ShapeDtypeStruct((b, nt, h, QK_DIM, tn), BF16))
    out_specs += [pl.BlockSpec((None, None, h, tn, QK_DIM), lambda bi, i: (bi, i, 0, 0, 0)),
                  pl.BlockSpec((None, None, h, V_HEAD, tn), lambda bi, i: (bi, i, 0, 0, 0))]
    out_shape += [jax.ShapeDtypeStruct((b, nt, h, tn, QK_DIM), BF16),
                  jax.ShapeDtypeStruct((b, nt, h, V_HEAD, tn), BF16)]
    return pl.pallas_call(
        functools.partial(_qkv_kernel, rope=rope, with_q=rope),
        grid=(b, nt),
        in_specs=in_specs,
        out_specs=out_specs,
        out_shape=out_shape,
        compiler_params=_cparams(("parallel", "parallel")),
        name="qkv_rope" if rope else "kv_ctx",
    )(*args)


def _attn_kernel(q_ref, kl_ref, kc_ref, vl_ref, vc_ref, o_ref, sa_ref, sb_ref):
    qt = q_ref[...]
    tq = qt.shape[1]
    nt = kl_ref.shape[0]

    def scores(kb):
        s = jnp.dot(kb, qt, preferred_element_type=F32)
        return s, jnp.max(s, axis=0, keepdims=True)

    def softmax_pv(s, smax, vb, carry):
        m, l, acc = carry
        m_new = jnp.maximum(m, smax)
        alpha = jnp.exp2(m - m_new)
        p = jnp.exp2(s - m_new)
        l = alpha * l + jnp.sum(p, axis=0, keepdims=True)
        acc = alpha * acc + jnp.dot(vb, p.astype(BF16), preferred_element_type=F32)
        return m_new, l, acc

    init = (jnp.full((1, tq), -jnp.inf, F32), jnp.zeros((1, tq), F32), jnp.zeros((V_HEAD, tq), F32))
    sa_ref[...], ma = scores(kl_ref[0])
    sc, mc = scores(kc_ref[0])
    carry = softmax_pv(sc, mc, vc_ref[0], init)

    def pair(j, c):
        ma, c = c[0], c[1:]
        sb_ref[...], mb = scores(kl_ref[2 * j + 1])
        c = softmax_pv(sa_ref[...], ma, vl_ref[2 * j], c)
        sa_ref[...], ma = scores(kl_ref[jnp.minimum(2 * j + 2, nt - 1)])
        return (ma,) + softmax_pv(sb_ref[...], mb, vl_ref[2 * j + 1], c)

    pairs = nt // 2
    unroll = 4 if pairs % 4 == 0 else (2 if pairs % 2 == 0 else 1)
    _, _, l, acc = lax.fori_loop(0, pairs, pair, (ma,) + carry, unroll=unroll)
    o_ref[...] = jnp.transpose(acc / l).astype(BF16)


def _attention(q_t, k_l, v_l, k_c, v_c, tq):
    b, nt, h, _, tn = q_t.shape
    n = nt * tn
    per = tn // tq
    tk = k_l.shape[3]
    tkc = k_c.shape[3]
    assert nt % 2 == 0, "key chunks are processed in pairs"
    return pl.pallas_call(
        _attn_kernel,
        grid=(b, h, n // tq),
        in_specs=[pl.BlockSpec((None, None, None, QK_DIM, tq), lambda bi, hi, qi: (bi, qi // per, hi, 0, qi % per)),
                  pl.BlockSpec((None, nt, None, tk, QK_DIM), lambda bi, hi, qi: (bi, 0, hi, 0, 0)),
                  pl.BlockSpec((None, 1, None, tkc, QK_DIM), lambda bi, hi, qi: (bi, 0, hi, 0, 0)),
                  pl.BlockSpec((None, nt, None, V_HEAD, tk), lambda bi, hi, qi: (bi, 0, hi, 0, 0)),
                  pl.BlockSpec((None, 1, None, V_HEAD, tkc), lambda bi, hi, qi: (bi, 0, hi, 0, 0))],
        out_specs=pl.BlockSpec((None, tq, V_HEAD), lambda bi, hi, qi: (bi, qi, hi)),
        out_shape=jax.ShapeDtypeStruct((b, n, h * V_HEAD), BF16),
        scratch_shapes=[pltpu.VMEM((tk, tq), F32), pltpu.VMEM((tk, tq), F32)],
        compiler_params=_cparams(("parallel", "parallel", "arbitrary")),
        name="attention",
    )(q_t, k_l, k_c, v_l, v_c)


_HALO = 16


_XBC_WIDTH = SSD_INNER + 2 * SSD_GROUPS * SSD_STATE
_CONV_COLS = 512


def _conv_kernel(p_ref, x_ref, n_ref, w_ref, b_ref, xg_ref, bg_ref, cg_ref, buf):
    i = pl.program_id(1)
    t = x_ref.shape[0]
    buf[0:_HALO, :] = jnp.where(i > 0, p_ref[...].astype(F32), 0.0)
    buf[_HALO:_HALO + t, :] = x_ref[...].astype(F32)
    buf[_HALO + t:, :] = jnp.where(i < pl.num_programs(1) - 1, n_ref[...].astype(F32), 0.0)
    for cb in range(_XBC_WIDTH // _CONV_COLS):
        cols = slice(cb * _CONV_COLS, (cb + 1) * _CONV_COLS)
        acc = b_ref[:, cols] + w_ref[0:1, cols] * buf[_HALO - 2:_HALO - 2 + t, cols]
        for k in range(1, SSD_CONV):
            acc = acc + w_ref[k:k + 1, cols] * buf[_HALO - 2 + k:_HALO - 2 + k + t, cols]
        y = _silu(acc).astype(BF16)
        if cb < SSD_GROUPS:
            xg_ref[cb] = y
        else:
            for q in range(_CONV_COLS // SSD_STATE):
                g = (cb - SSD_GROUPS) * (_CONV_COLS // SSD_STATE) + q
                piece = y[:, q * SSD_STATE:(q + 1) * SSD_STATE]
                if g < SSD_GROUPS:
                    bg_ref[g] = piece
                else:
                    cg_ref[g - SSD_GROUPS] = piece


def _conv(u_main, conv_w, conv_b, tc):
    b, n, _ = u_main.shape
    nh = n // _HALO
    per = tc // _HALO
    c0 = U_XBC // _XBC_WIDTH
    assert c0 * _XBC_WIDTH == U_XBC
    g = SSD_GROUPS
    return pl.pallas_call(
        _conv_kernel,
        grid=(b, n // tc),
        in_specs=[pl.BlockSpec((None, _HALO, _XBC_WIDTH), lambda bi, i: (bi, jnp.maximum(i * per - 1, 0), c0)),
                  pl.BlockSpec((None, tc, _XBC_WIDTH), lambda bi, i: (bi, i, c0)),
                  pl.BlockSpec((None, _HALO, _XBC_WIDTH), lambda bi, i: (bi, jnp.minimum((i + 1) * per, nh - 1), c0)),
                  pl.BlockSpec((SSD_CONV, _XBC_WIDTH), lambda bi, i: (0, 0)),
                  pl.BlockSpec((1, _XBC_WIDTH), lambda bi, i: (0, 0))],
        out_specs=[pl.BlockSpec((None, g, tc, GROUP_WIDTH), lambda bi, i: (bi, 0, i, 0)),
                   pl.BlockSpec((None, g, tc, SSD_STATE), lambda bi, i: (bi, 0, i, 0)),
                   pl.BlockSpec((None, g, tc, SSD_STATE), lambda bi, i: (bi, 0, i, 0))],
        out_shape=[jax.ShapeDtypeStruct((b, g, n, GROUP_WIDTH), BF16),
                   jax.ShapeDtypeStruct((b, g, n, SSD_STATE), BF16),
                   jax.ShapeDtypeStruct((b, g, n, SSD_STATE), BF16)],
        scratch_shapes=[pltpu.VMEM((tc + 2 * _HALO, _XBC_WIDTH), F32)],
        compiler_params=_cparams(("parallel", "arbitrary")),
        name="conv",
    )(u_main, u_main, u_main, conv_w, conv_b.reshape(1, -1))


def _dt_kernel(dt_ref, bias_ref, a_ref, dtb_ref, dob_ref, dsb_ref, cst_ref, dc_ref):
    q = dt_ref.shape[0]
    dt = jax.nn.softplus(dt_ref[...] + bias_ref[...])
    a = dt * a_ref[...]
    row = lax.broadcasted_iota(jnp.int32, a.shape, 0)
    cs = a
    s = 1
    while s < q:
        cs = cs + jnp.where(row >= s, pltpu.roll(cs, s, 0), 0.0)
        s *= 2
    total = cs[q - 1:q, :]
    lane = lax.broadcasted_iota(jnp.int32, a.shape, 1)
    cs = jnp.where(lane < SSD_HEADS, cs, total - cs + a)
    dtb_ref[...] = dt.astype(BF16)
    dob_ref[...] = jnp.exp(cs).astype(BF16)
    dsb_ref[...] = jnp.exp(total - cs).astype(BF16)
    cst_ref[...] = jnp.transpose(cs)
    dc_ref[...] = jnp.broadcast_to(jnp.exp(total), dc_ref.shape)


def _dt_prep(u_small, dt_bias, a_coef):
    b, n, _ = u_small.shape
    q = SSD_CHUNK
    nc = n // q
    vec = lambda: pl.BlockSpec((1, 128), lambda bi, c: (0, 0))
    tok = lambda: pl.BlockSpec((None, q, 128), lambda bi, c: (bi, c, 0))
    return pl.pallas_call(
        _dt_kernel,
        grid=(b, nc),
        in_specs=[tok(), vec(), vec()],
        out_specs=[tok(), tok(), tok(),
                   pl.BlockSpec((None, None, 128, q), lambda bi, c: (bi, c, 0, 0)),
                   pl.BlockSpec((None, None, 8, 128), lambda bi, c: (bi, c, 0, 0))],
        out_shape=[jax.ShapeDtypeStruct((b, n, 128), BF16)] * 3
        + [jax.ShapeDtypeStruct((b, nc, 128, q), F32), jax.ShapeDtypeStruct((b, nc, 8, 128), F32)],
        compiler_params=_cparams(("parallel", "parallel")),
        name="dt_prep",
    )(u_small, dt_bias, a_coef)


_TN = (((0,), (0,)), ((), ()))


def _ssd_kernel(x_ref, b_ref, c_ref, dt_ref, do_ref, ds_ref, cst_ref, dc_ref, e_ref, init_ref,
                y_ref, fin_ref, st_ref, *, direction, cpb):
    q = SSD_CHUNK
    step = pl.program_id(1)

    @pl.when(step == 0)
    def _():
        st_ref[...] = init_ref[...]

    ri = lax.broadcasted_iota(jnp.int32, (q, q), 0)
    ci = lax.broadcasted_iota(jnp.int32, (q, q), 1)
    mask = (ri >= ci) if direction == 0 else (ri <= ci)
    lane = lax.broadcasted_iota(jnp.int32, (q, 2 * SSD_HEADDIM), 1)
    left = lane < SSD_HEADDIM

    def chunk_body(cc, carry):
        c = cc if direction == 0 else cpb - 1 - cc
        rows = pl.ds(pl.multiple_of(c * q, q), q)
        dtb = dt_ref[rows, :]
        dob = do_ref[rows, :]
        dsb = ds_ref[rows, :]

        def group_body(g, carry2):
            eg = e_ref[g]
            dtx = jnp.dot(dtb, eg, preferred_element_type=F32)
            dox = jnp.dot(dob, eg, preferred_element_type=F32)
            dsx = jnp.dot(dsb, eg, preferred_element_type=F32)
            xdt = x_ref[g, rows, :].astype(F32) * dtx
            xdt_b = xdt.astype(BF16)
            xds_b = (xdt * dsx).astype(BF16)
            bg = b_ref[g, rows, :]
            cg = c_ref[g, rows, :]
            cb = lax.dot_general(cg, bg, _NT, preferred_element_type=F32)
            st = st_ref[g]
            y_off = jnp.dot(cg, st.astype(BF16), preferred_element_type=F32) * dox
            g8 = cst_ref[c, pl.ds(pl.multiple_of(direction * SSD_HEADS + g * HEADS_PER_GROUP, 8), 8), :]
            g8t = jnp.transpose(g8)
            pieces = []
            for pr in range(HEADS_PER_GROUP // 2):
                pair = xdt_b[:, pr * 128:(pr + 1) * 128]
                acc = None
                for half in range(2):
                    r = 2 * pr + half
                    seg = g8t[:, r:r + 1] - g8[r:r + 1, :]
                    m = (cb * jnp.exp(jnp.where(mask, seg, -jnp.inf))).astype(BF16)
                    xin = jnp.where(left if half == 0 else jnp.logical_not(left), pair, jnp.zeros_like(pair))
                    d = jnp.dot(m, xin, preferred_element_type=F32)
                    acc = d if acc is None else acc + d
                pieces.append(acc)
            y = jnp.concatenate(pieces, axis=1) + y_off
            y_ref[g, rows, :] = y.astype(BF16)
            st_new = lax.dot_general(bg, xds_b, _TN, preferred_element_type=F32)
            st_ref[g] = st * dc_ref[c, pl.ds(g, 1), :] + st_new
            return carry2

        lax.fori_loop(0, SSD_GROUPS, group_body, 0, unroll=4)
        return carry

    lax.fori_loop(0, cpb, chunk_body, 0)

    @pl.when(step == pl.num_programs(1) - 1)
    def _():
        fin_ref[...] = st_ref[...]


def _ssd(xg, bg, cg, dtb, dob, dsb, cst, dcx, e_mat, init, direction, cpb):
    b, g, n, _ = xg.shape
    q = SSD_CHUNK
    t = q * cpb
    nt = n // t
    ti = (lambda i: i) if direction == 0 else (lambda i: nt - 1 - i)
    grp = lambda w: pl.BlockSpec((None, g, t, w), lambda bi, i: (bi, 0, ti(i), 0))
    tok = lambda: pl.BlockSpec((None, t, 128), lambda bi, i: (bi, ti(i), 0))
    return pl.pallas_call(
        functools.partial(_ssd_kernel, direction=direction, cpb=cpb),
        grid=(b, nt),
        in_specs=[grp(GROUP_WIDTH), grp(SSD_STATE), grp(SSD_STATE), tok(), tok(), tok(),
                  pl.BlockSpec((None, cpb, 128, q), lambda bi, i: (bi, ti(i), 0, 0)),
                  pl.BlockSpec((None, cpb, None, g, GROUP_WIDTH), lambda bi, i: (bi, ti(i), direction, 0, 0)),
                  pl.BlockSpec((None, g, 128, GROUP_WIDTH), lambda bi, i: (direction, 0, 0, 0)),
                  pl.BlockSpec((None, g, SSD_STATE, GROUP_WIDTH), lambda bi, i: (bi, 0, 0, 0))],
        out_specs=[grp(GROUP_WIDTH),
                   pl.BlockSpec((None, g, SSD_STATE, GROUP_WIDTH), lambda bi, i: (bi, 0, 0, 0))],
        out_shape=[jax.ShapeDtypeStruct((b, g, n, GROUP_WIDTH), BF16),
                   jax.ShapeDtypeStruct((b, g, SSD_STATE, GROUP_WIDTH), F32)],
        scratch_shapes=[pltpu.VMEM((g, SSD_STATE, GROUP_WIDTH), F32)],
        compiler_params=_cparams(("parallel", "arbitrary")),
        name="ssd_fwd" if direction == 0 else "ssd_bwd",
    )(xg, bg, cg, dtb, dob, dsb, cst, dcx, e_mat, init)


def _ssd_out_kernel(x_ref, yf_ref, yb_ref, z0_ref, z1_ref, d_ref, g_ref, w_ref, o_ref):
    ys = []
    ss = None
    half = SSD_GROUPS // 2
    for g in range(SSD_GROUPS):
        cols = slice(g * GROUP_WIDTH, (g + 1) * GROUP_WIDTH)
        z_ref = z0_ref if g < half else z1_ref
        zcols = slice((g % half) * GROUP_WIDTH, (g % half + 1) * GROUP_WIDTH)
        y = d_ref[:, cols] * x_ref[g].astype(F32) + yf_ref[g].astype(F32) + yb_ref[g].astype(F32)
        y = y * _silu(z_ref[:, zcols].astype(F32))
        sq = jnp.sum(y * y, axis=-1, keepdims=True)
        ss = sq if ss is None else ss + sq
        ys.append(y)
    inv = lax.rsqrt(ss * (1.0 / SSD_INNER) + NORM_EPS)
    acc = None
    for g in range(SSD_GROUPS):
        cols = slice(g * GROUP_WIDTH, (g + 1) * GROUP_WIDTH)
        yn = (ys[g] * inv * g_ref[:, cols]).astype(BF16)
        d = jnp.dot(yn, w_ref[cols, :], preferred_element_type=F32)
        acc = d if acc is None else acc + d
    o_ref[...] = acc.astype(BF16)


def _ssd_out(xg, yf, yb, u_main, d_exp, g_ssd, w, tm):
    b, g, n, gw = xg.shape
    dm = w.shape[1]
    grp = lambda: pl.BlockSpec((None, g, tm, gw), lambda bi, i: (bi, 0, i, 0))
    return pl.pallas_call(
        _ssd_out_kernel,
        grid=(b, n // tm),
        in_specs=[grp(), grp(), grp(),
                  pl.BlockSpec((None, tm, HALF_INNER), lambda bi, i: (bi, i, U_Z // HALF_INNER)),
                  pl.BlockSpec((None, tm, HALF_INNER), lambda bi, i: (bi, i, U_Z // HALF_INNER + 1)),
                  pl.BlockSpec((1, SSD_INNER), lambda bi, i: (0, 0)),
                  pl.BlockSpec((1, SSD_INNER), lambda bi, i: (0, 0)),
                  pl.BlockSpec((SSD_INNER, dm), lambda bi, i: (0, 0))],
        out_specs=pl.BlockSpec((None, tm, dm), lambda bi, i: (bi, i, 0)),
        out_shape=jax.ShapeDtypeStruct((b, n, dm), BF16),
        compiler_params=_cparams(("parallel", "parallel")),
        name="ssd_out",
    )(xg, yf, yb, u_main, u_main, d_exp, g_ssd, w)


def _pack_pairs(x):
    k = x.shape[1] // 2
    u = lax.bitcast_convert_type(x.astype(BF16).astype(F32), U32)
    return (u[:, k:] & jnp.uint32(0xFFFF0000)) | (u[:, :k] >> 16)


def _unpack_pairs(u):
    lo = lax.bitcast_convert_type(u << 16, F32)
    hi = lax.bitcast_convert_type(u & jnp.uint32(0xFFFF0000), F32)
    return jnp.concatenate([lo, hi], axis=1)


_ROUTER_LANES = 128


def _route(lg):
    lane = lax.broadcasted_iota(jnp.int32, lg.shape, 1).astype(F32)
    big = float(_ROUTER_LANES)

    def first_max(v):
        mx = jnp.max(v, axis=-1, keepdims=True)
        return mx, jnp.min(jnp.where(v == mx, lane, big), axis=-1, keepdims=True)

    gl = jnp.where(lane < N_GROUPS_E, lg, -jnp.inf)
    gmax, g_sel = first_max(gl)
    g_top = 1.0 / jnp.sum(jnp.exp(gl - gmax), axis=-1, keepdims=True)
    lo = N_GROUPS_E + EXPERTS_PER_GROUP * g_sel
    el = jnp.where((lane >= lo) & (lane < lo + EXPERTS_PER_GROUP), lg, -jnp.inf)
    e1, i1 = first_max(el)
    e2, i2 = first_max(jnp.where(lane == i1, -jnp.inf, el))
    r = jnp.exp(e2 - e1)
    w1 = g_top / (1.0 + r)
    w2 = w1 * r
    ids = jnp.where(lane == 0.0, i1 - N_GROUPS_E, jnp.where(lane == 1.0, i2 - N_GROUPS_E, 0.0)).astype(jnp.int32)
    wts = jnp.where(lane == 0.0, w1, jnp.where(lane == 1.0, w2, 0.0))
    return ids, wts


def _merge_kernel(s_ref, o_ref, gs_ref, gm_ref, x_ref, ga_ref, shf_ref, scf_ref, gpost_ref, gpre_ref,
                  wo_ref, wout_ref, wrh_ref, wrl_ref, br_ref, x1_ref, hp_ref, id_ref, wt_ref):
    y_mla = jnp.dot(o_ref[...], wo_ref[...], preferred_element_type=F32)
    mix = (jax.nn.sigmoid(gs_ref[...].astype(F32)) * s_ref[...].astype(F32)
           + jax.nn.sigmoid(gm_ref[...].astype(F32)) * y_mla)
    ym = jnp.dot(mix.astype(BF16), wout_ref[...], preferred_element_type=F32)
    x1 = x_ref[...] + ga_ref[...] * (_rms(ym) * gpost_ref[...])
    x1_ref[...] = x1
    h2 = _rms(x1) * (gpre_ref[...] * (1.0 + scf_ref[...])) + shf_ref[...]
    hp_ref[...] = _pack_pairs(h2)
    hh = h2.astype(BF16)
    hl = (h2 - hh.astype(F32)).astype(BF16)
    lg = (jnp.dot(hh, wrh_ref[...], preferred_element_type=F32)
          + jnp.dot(hh, wrl_ref[...], preferred_element_type=F32)
          + jnp.dot(hl, wrh_ref[...], preferred_element_type=F32))
    ids, wts = _route(lg + br_ref[...])
    id_ref[...] = ids
    wt_ref[...] = wts


def _merge(s, o, u_main, x, g_a, sh_f, sc_f, g_post, g_pre, w_o, w_out, wr_hi, wr_lo, b_r, tm):
    b, n, d = x.shape
    tokd = lambda: pl.BlockSpec((None, tm, d), lambda bi, i: (bi, i, 0))
    modv = lambda: pl.BlockSpec((None, 1, d), lambda bi, i: (bi, 0, 0))
    vec = lambda w: pl.BlockSpec((1, w), lambda bi, i: (0, 0))
    mat = lambda r, c: pl.BlockSpec((r, c), lambda bi, i: (0, 0))
    return pl.pallas_call(
        _merge_kernel,
        grid=(b, n // tm),
        in_specs=[tokd(), tokd(),
                  pl.BlockSpec((None, tm, d), lambda bi, i: (bi, i, U_GATE // d)),
                  pl.BlockSpec((None, tm, d), lambda bi, i: (bi, i, U_GATE // d + 1)),
                  tokd(), modv(), modv(), modv(), vec(d), vec(d),
                  mat(d, d), mat(d, d), mat(d, _ROUTER_LANES), mat(d, _ROUTER_LANES), vec(_ROUTER_LANES)],
        out_specs=[tokd(),
                   pl.BlockSpec((None, tm, d // 2), lambda bi, i: (bi, i, 0)),
                   pl.BlockSpec((None, tm, _ROUTER_LANES), lambda bi, i: (bi, i, 0)),
                   pl.BlockSpec((None, tm, _ROUTER_LANES), lambda bi, i: (bi, i, 0))],
        out_shape=[jax.ShapeDtypeStruct((b, n, d), F32),
                   jax.ShapeDtypeStruct((b, n, d // 2), U32),
                   jax.ShapeDtypeStruct((b, n, _ROUTER_LANES), jnp.int32),
                   jax.ShapeDtypeStruct((b, n, _ROUTER_LANES), F32)],
        compiler_params=_cparams(("parallel", "parallel")),
        name="merge",
    )(s, o, u_main, u_main, x, g_a, sh_f, sc_f, g_post, g_pre, w_o, w_out, wr_hi, wr_lo, b_r)


_GATHER_UNROLL = 8


def _row_copy(src_ref, dst_ref, sem, src_row, dst_row):
    return pltpu.make_async_copy(src_ref.at[pl.ds(src_row, 1), :], dst_ref.at[pl.ds(dst_row, 1), :], sem)


def _gather_start(idx_ref, base, src_ref, dst_ref, sem):
    def issue(r, carry):
        _row_copy(src_ref, dst_ref, sem, idx_ref[base + r], r).start()
        return carry

    lax.fori_loop(0, dst_ref.shape[0], issue, 0, unroll=_GATHER_UNROLL)


def _gather_wait(src_ref, dst_ref, sem):
    def drain(r, carry):
        _row_copy(src_ref, dst_ref, sem, 0, r).wait()
        return carry

    lax.fori_loop(0, dst_ref.shape[0], drain, 0, unroll=_GATHER_UNROLL)


def _expert_kernel(be_ref, nu_ref, src_ref, tid_ref, h_ref, wg_ref, wu_ref, wd_ref, y_ref, x_buf, sem):
    i = pl.program_id(0)
    n_used = nu_ref[0]
    slot = i % 2

    def start_block(blk, dst_slot):
        base = src_ref[blk]
        for r in range(MOE_BLOCK):
            _row_copy(h_ref, x_buf.at[dst_slot], sem.at[dst_slot], tid_ref[base + r], r).start()

    def wait_block(dst_slot):
        for r in range(MOE_BLOCK):
            _row_copy(h_ref, x_buf.at[dst_slot], sem.at[dst_slot], 0, r).wait()

    @pl.when(i == 0)
    def _():
        start_block(0, 0)

    @pl.when(i < n_used)
    def _():
        wait_block(slot)
        start_block(i + 1, 1 - slot)
        x = _unpack_pairs(x_buf[slot]).astype(BF16)
        hg = jnp.dot(x, wg_ref[...], preferred_element_type=F32)
        hu = jnp.dot(x, wu_ref[...], preferred_element_type=F32)
        a = (_silu(hg) * hu).astype(BF16)
        y_ref[...] = _pack_pairs(jnp.dot(a, wd_ref[...], preferred_element_type=F32))

    @pl.when(i == n_used)
    def _():
        wait_block(slot)

    @pl.when(i >= n_used)
    def _():
        y_ref[...] = jnp.zeros(y_ref.shape, y_ref.dtype)


def _experts(h_pack, tid_sorted, blk_e, blk_src, n_used, w_gate, w_up, w_down):
    nb = blk_e.shape[0]
    dh = h_pack.shape[1]
    ne, d, ff = w_gate.shape
    wspec = lambda r, c: pl.BlockSpec((None, r, c), lambda i, be, nu, src, tid: (be[i], 0, 0))
    return pl.pallas_call(
        _expert_kernel,
        grid_spec=pltpu.PrefetchScalarGridSpec(
            num_scalar_prefetch=4,
            grid=(nb,),
            in_specs=[pl.BlockSpec(memory_space=pl.ANY), wspec(d, ff), wspec(d, ff), wspec(ff, d)],
            out_specs=pl.BlockSpec((MOE_BLOCK, dh), lambda i, be, nu, src, tid: (i, 0)),
            scratch_shapes=[pltpu.VMEM((2, MOE_BLOCK, dh), U32), pltpu.SemaphoreType.DMA((2,))]),
        out_shape=jax.ShapeDtypeStruct((nb * MOE_BLOCK, dh), U32),
        compiler_params=_cparams(("arbitrary",)),
        name="experts",
    )(blk_e, n_used, blk_src, tid_sorted, h_pack, w_gate, w_up, w_down)


def _final_kernel(p0_ref, p1_ref, y_ref, w_ref, x1_ref, gf_ref, gpost_ref, o_ref, y0_buf, y1_buf, sem0, sem1):
    tm = y0_buf.shape[0]
    base = (pl.program_id(0) * pl.num_programs(1) + pl.program_id(1)) * tm
    _gather_start(p0_ref, base, y_ref, y0_buf, sem0)
    _gather_start(p1_ref, base, y_ref, y1_buf, sem1)
    _gather_wait(y_ref, y0_buf, sem0)
    _gather_wait(y_ref, y1_buf, sem1)
    w = w_ref[...]
    f = w[:, 0:1] * _unpack_pairs(y0_buf[...]) + w[:, 1:2] * _unpack_pairs(y1_buf[...])
    o_ref[...] = x1_ref[...] + gf_ref[...] * (_rms(f) * gpost_ref[...])


def _final(y_buf, pos0, pos1, wts, x1, g_f, g_post, tm):
    b, n, d = x1.shape
    return pl.pallas_call(
        _final_kernel,
        grid_spec=pltpu.PrefetchScalarGridSpec(
            num_scalar_prefetch=2,
            grid=(b, n // tm),
            in_specs=[pl.BlockSpec(memory_space=pl.ANY),
                      pl.BlockSpec((None, tm, 128), lambda bi, i, p0, p1: (bi, i, 0)),
                      pl.BlockSpec((None, tm, d), lambda bi, i, p0, p1: (bi, i, 0)),
                      pl.BlockSpec((None, 1, d), lambda bi, i, p0, p1: (bi, 0, 0)),
                      pl.BlockSpec((1, d), lambda bi, i, p0, p1: (0, 0))],
            out_specs=pl.BlockSpec((None, tm, d), lambda bi, i, p0, p1: (bi, i, 0)),
            scratch_shapes=[pltpu.VMEM((tm, d // 2), U32), pltpu.VMEM((tm, d // 2), U32),
                            pltpu.SemaphoreType.DMA(()), pltpu.SemaphoreType.DMA(())]),
        out_shape=jax.ShapeDtypeStruct((b, n, d), F32),
        compiler_params=_cparams(("arbitrary", "arbitrary")),
        name="final",
    )(pos0, pos1, y_buf, wts, x1, g_f, g_post)


def _lookup(table, idx):
    e = jnp.arange(table.shape[0], dtype=jnp.int32)
    return jnp.sum(jnp.where(idx[:, None] == e[None, :], table[None, :], 0), axis=1)


def _dispatch_plan(expert_id):
    t = expert_id.shape[0]
    n_assign = t * TOP_K
    eid = expert_id.reshape(n_assign)
    eid_s, order = lax.sort((eid, jnp.arange(n_assign, dtype=jnp.int32)), num_keys=1)
    experts = jnp.arange(N_EXPERTS, dtype=jnp.int32)
    start = jnp.sum((eid_s[None, :] < experts[:, None]).astype(jnp.int32), axis=1)
    counts = jnp.concatenate([start[1:], jnp.full((1,), n_assign, jnp.int32)]) - start
    nblk_e = (counts + MOE_BLOCK - 1) // MOE_BLOCK
    bend = jnp.cumsum(nblk_e)
    pstart = (bend - nblk_e) * MOE_BLOCK
    n_used = bend[-1].astype(jnp.int32)
    n_blocks = n_assign // MOE_BLOCK + N_EXPERTS + 1
    blk = jnp.arange(n_blocks, dtype=jnp.int32)
    blk_e = jnp.sum((bend[None, :] <= jnp.minimum(blk, n_used - 1)[:, None]).astype(jnp.int32), axis=1)
    blk_e = jnp.clip(blk_e, 0, N_EXPERTS - 1)
    blk_src = _lookup(start - pstart, blk_e) + blk * MOE_BLOCK
    blk_src = jnp.where(blk < n_used, blk_src, 0).astype(jnp.int32)
    dest_s = jnp.arange(n_assign, dtype=jnp.int32) + _lookup(pstart - start, eid_s)
    _, pos = lax.sort((order, dest_s), num_keys=1)
    tid_sorted = jnp.concatenate([order // TOP_K, jnp.zeros((MOE_BLOCK,), jnp.int32)])
    return pos, tid_sorted, blk_e.astype(jnp.int32), blk_src, n_used.reshape(1)


def _rope_tables(n):
    rows = n // GRID_W
    row = jnp.repeat(jnp.arange(rows), GRID_W).astype(F32)
    col = jnp.tile(jnp.arange(GRID_W), rows).astype(F32)
    half = QK_ROPE // 2
    freqs = ROPE_THETA ** (-jnp.arange(0, half, 2, dtype=F32) / half)
    ar = row[:, None] * freqs
    ac = col[:, None] * freqs
    cos = jnp.concatenate([jnp.cos(ar), jnp.cos(ar), jnp.cos(ac), jnp.cos(ac)], axis=1)
    sin = jnp.concatenate([-jnp.sin(ar), jnp.sin(ar), -jnp.sin(ac), jnp.sin(ac)], axis=1)
    return cos.T, sin.T, cos, sin


def _pick(n, pref):
    return pref if n % pref == 0 else n


def kernel(x, c, ctx, c_ctx, w_ada, b_ada, g_pre_mix, g_post_mix, g_pre_ffn, g_post_ffn, w_in, g_q, w_uq,
           g_kv, w_ukv, w_o_mla, conv_w, conv_b, a_log, dt_bias, d_skip, g_ssd, w_ssd_out, w_out,
           w_router_group, b_router_group, w_router_expert, b_router_expert, w_exp_gate, w_exp_up, w_exp_down):
    assert w_ada.shape[0] == 1, "single-layer block"
    b, n, d = x.shape
    n_ctx = ctx.shape[1]
    h = MLA_HEADS

    w_in0 = w_in[0]
    o_ckv = Q_LORA
    o_kr = o_ckv + KV_LORA
    o_z = o_kr + QK_ROPE
    o_xbc = o_z + SSD_INNER
    o_dt = o_xbc + SSD_INNER + 2 * SSD_GROUPS * SSD_STATE
    o_gate = o_dt + 2 * SSD_HEADS
    w_kr = w_in0[:, o_kr:o_z]
    swap = jnp.concatenate([jnp.arange(16, 32), jnp.arange(0, 16), jnp.arange(48, 64), jnp.arange(32, 48)])
    assert d == HALF_INNER
    w_main = jnp.concatenate([w_in0[:, o_xbc:o_dt], w_in0[:, o_z:o_xbc], w_in0[:, o_gate:],
                              w_in0[:, :o_ckv], w_in0[:, o_ckv:o_kr]], axis=1).astype(BF16)
    w_small = jnp.concatenate([w_in0[:, o_dt:o_gate], w_kr, w_kr[:, swap]], axis=1).astype(BF16)
    wq_t = w_uq[0].reshape(Q_LORA, h, QK_DIM).transpose(1, 2, 0).astype(BF16)
    wkv = w_ukv[0].reshape(KV_LORA, h, QK_NOPE + V_HEAD)
    wk = wkv[:, :, :QK_NOPE].transpose(1, 0, 2).astype(BF16)
    wv_t = wkv[:, :, QK_NOPE:].transpose(1, 2, 0).astype(BF16)
    w_o = w_o_mla[0].astype(BF16)
    w_so = w_ssd_out[0].astype(BF16)
    w_mix = w_out[0].astype(BF16)
    w_r = jnp.concatenate([w_router_group[0], w_router_expert[0],
                           jnp.zeros((d, 128 - N_GROUPS_E - N_EXPERTS), F32)], axis=1)
    wr_hi = w_r.astype(BF16)
    wr_lo = (w_r - wr_hi.astype(F32)).astype(BF16)
    b_r = jnp.concatenate([b_router_group[0], b_router_expert[0],
                           jnp.zeros((128 - N_GROUPS_E - N_EXPERTS,), F32)]).reshape(1, 128)
    wg = w_exp_gate[0].astype(BF16)
    wu = w_exp_up[0].astype(BF16)
    wd = w_exp_down[0].astype(BF16)
    vec = lambda v: v.reshape(1, -1)

    c8 = jnp.concatenate([c, c_ctx[None, :], jnp.zeros((8 - b - 1, d), F32)], axis=0)
    mod = _ada(c8, w_ada[0], b_ada[0])
    sh_a, sc_a, g_a, sh_f, sc_f, g_f = [mod[:, i * d:(i + 1) * d] for i in range(N_MOD)]
    lat = lambda m: m[:b].reshape(b, 1, d)
    cmod = lambda m: jnp.broadcast_to(m[b:b + 1].reshape(1, 1, d), (b, 1, d))

    u_main, u_small = _inproj(x, lat(sc_a), lat(sh_a), vec(g_pre_mix[0]), w_main, w_small, _pick(n, 512))
    uc_main, uc_small = _inproj(ctx, cmod(sc_a), cmod(sh_a), vec(g_pre_mix[0]), w_main, w_small, n_ctx)

    tn = _pick(n, 1024)
    q_t, k_l, v_l = _qkv(u_main, u_small, vec(g_q[0]), vec(g_kv[0]), wq_t, wk, wv_t, _rope_tables(n), tn)
    k_c, v_c = _qkv(uc_main, uc_small, None, vec(g_kv[0]), None, wk, wv_t, None, n_ctx)
    o_mla = _attention(q_t, k_l, v_l, k_c, v_c, _pick(tn, 512))

    a_coef = -jnp.exp(a_log[0].astype(F32)).reshape(1, 2 * SSD_HEADS)
    dtb_flat = dt_bias[0].astype(F32).reshape(1, 2 * SSD_HEADS)
    hh = jnp.arange(2 * SSD_HEADS)
    lane_head = jnp.arange(GROUP_WIDTH) // SSD_HEADDIM
    e_mat = (hh[None, None, :, None] == (jnp.arange(2)[:, None, None, None] * SSD_HEADS
                                        + jnp.arange(SSD_GROUPS)[None, :, None, None] * HEADS_PER_GROUP
                                        + lane_head[None, None, None, :])).astype(BF16)

    def ssd_inputs(um, us, tc):
        xg, bg, cg = _conv(um, conv_w[0], conv_b[0], tc)
        dtb, dob, dsb, cst, dc = _dt_prep(us, dtb_flat, a_coef)
        nc = dc.shape[1]
        dcx = jnp.repeat(dc[:, :, 0, :].reshape(b, nc, 2, SSD_GROUPS, HEADS_PER_GROUP), SSD_HEADDIM, axis=-1)
        return xg, bg, cg, dtb, dob, dsb, cst, dcx

    zero_state = jnp.zeros((b, SSD_GROUPS, SSD_STATE, GROUP_WIDTH), F32)
    ctx_in = ssd_inputs(uc_main, uc_small, n_ctx)
    lat_in = ssd_inputs(u_main, u_small, _pick(n, 256))
    cpb_c = n_ctx // SSD_CHUNK
    cpb = _pick(n, 512) // SSD_CHUNK
    ys = []
    for direction in range(2):
        _, s0 = _ssd(*ctx_in, e_mat, zero_state, direction, cpb_c)
        y_dir, _ = _ssd(*lat_in, e_mat, s0, direction, cpb)
        ys.append(y_dir)
    d_exp = jnp.repeat(d_skip[0].astype(F32), SSD_HEADDIM).reshape(1, SSD_INNER)
    s_l = _ssd_out(lat_in[0], ys[0], ys[1], u_main, d_exp, vec(g_ssd[0]), w_so, _pick(n, 256))

    x1, h_pack, ids, wts = _merge(s_l, o_mla, u_main, x, lat(g_a), lat(sh_f), lat(sc_f), vec(g_post_mix[0]),
                                  vec(g_pre_ffn[0]), w_o, w_mix, wr_hi, wr_lo, b_r, _pick(n, 256))

    t = b * n
    pos, tid_sorted, blk_e, blk_src, n_used = _dispatch_plan(ids.reshape(t, _ROUTER_LANES)[:, :TOP_K])
    y_buf = _experts(h_pack.reshape(t, d // 2), tid_sorted, blk_e, blk_src, n_used, wg, wu, wd)
    pos = pos.reshape(t, TOP_K)
    return _final(y_buf, pos[:, 0], pos[:, 1], wts, x1, lat(g_f), vec(g_post_ffn[0]), _pick(n, 256))
```

```python
import functools
import math

import jax
import jax.numpy as jnp
from jax import lax
from jax.experimental import pallas as pl
from jax.experimental.pallas import tpu as pltpu

F32 = jnp.float32
BF16 = jnp.bfloat16
U32 = jnp.uint32

GRID_W = 64
N_MOD = 6
NORM_EPS = 1e-6
MLA_HEADS = 16
Q_LORA = 512
KV_LORA = 512
QK_NOPE = 128
QK_ROPE = 64
QK_DIM = QK_NOPE + QK_ROPE
V_HEAD = 128
ROPE_THETA = 10000.0
ATTN_SCALE = QK_DIM ** -0.5
SSD_HEADDIM = 64
SSD_HEADS = 64
SSD_INNER = SSD_HEADS * SSD_HEADDIM
SSD_GROUPS = 8
SSD_STATE = 128
SSD_CONV = 5
HEADS_PER_GROUP = SSD_HEADS // SSD_GROUPS
GROUP_WIDTH = HEADS_PER_GROUP * SSD_HEADDIM
N_GROUPS_E = 8
EXPERTS_PER_GROUP = 8
N_EXPERTS = N_GROUPS_E * EXPERTS_PER_GROUP
TOP_K = 2
LOG2E = 1.4426950408889634

U_XBC = 0
U_Z = U_XBC + SSD_INNER + 2 * SSD_GROUPS * SSD_STATE
U_GATE = U_Z + SSD_INNER
U_CQ = U_GATE + 2 * 2048
U_CKV = U_CQ + Q_LORA
U_MAIN = U_CKV + KV_LORA
HALF_INNER = SSD_INNER // 2
U_SMALL = 256

SSD_CHUNK = 128
MOE_BLOCK = 256
VMEM_LIMIT = 56 * 1024 * 1024


def _cparams(sem):
    return pltpu.CompilerParams(dimension_semantics=sem, vmem_limit_bytes=VMEM_LIMIT)


def _silu(x):
    return x * jax.nn.sigmoid(x)


def _rms(x, eps=NORM_EPS):
    return x * lax.rsqrt(jnp.mean(x * x, axis=-1, keepdims=True) + eps)


def _ada_kernel(c_ref, w_ref, b_ref, o_ref):
    c = c_ref[...]
    o_ref[...] = jnp.dot(_silu(c).astype(BF16), w_ref[...].astype(BF16),
                         preferred_element_type=F32) + b_ref[...]


def _ada(c8, w_ada, b_ada):
    d, n = w_ada.shape
    tn = 1536
    return pl.pallas_call(
        _ada_kernel,
        grid=(n // tn,),
        in_specs=[pl.BlockSpec((8, d), lambda j: (0, 0)),
                  pl.BlockSpec((d, tn), lambda j: (0, j)),
                  pl.BlockSpec((1, tn), lambda j: (0, j))],
        out_specs=pl.BlockSpec((8, tn), lambda j: (0, j)),
        out_shape=jax.ShapeDtypeStruct((8, n), F32),
        compiler_params=_cparams(("parallel",)),
        name="ada",
    )(c8, w_ada, b_ada.reshape(1, n))


def _inproj_kernel(x_ref, sc_ref, sh_ref, g_ref, wm_ref, ws_ref, om_ref, os_ref, h_scr):
    @pl.when(pl.program_id(2) == 0)
    def _():
        y = _rms(x_ref[...]) * (g_ref[...] * (1.0 + sc_ref[...])) + sh_ref[...]
        hb = y.astype(BF16)
        h_scr[...] = hb
        os_ref[...] = jnp.dot(hb, ws_ref[...], preferred_element_type=F32)

    om_ref[...] = jnp.dot(h_scr[...], wm_ref[...], preferred_element_type=F32).astype(BF16)


def _inproj(x, scale, shift, g, w_main, w_small, tm):
    b, n, d = x.shape
    tn = 1536
    return pl.pallas_call(
        _inproj_kernel,
        grid=(b, n // tm, U_MAIN // tn),
        in_specs=[pl.BlockSpec((None, tm, d), lambda bi, i, j: (bi, i, 0)),
                  pl.BlockSpec((None, 1, d), lambda bi, i, j: (bi, 0, 0)),
                  pl.BlockSpec((None, 1, d), lambda bi, i, j: (bi, 0, 0)),
                  pl.BlockSpec((1, d), lambda bi, i, j: (0, 0)),
                  pl.BlockSpec((d, tn), lambda bi, i, j: (0, j)),
                  pl.BlockSpec((d, U_SMALL), lambda bi, i, j: (0, 0))],
        out_specs=[pl.BlockSpec((None, tm, tn), lambda bi, i, j: (bi, i, j)),
                   pl.BlockSpec((None, tm, U_SMALL), lambda bi, i, j: (bi, i, 0))],
        out_shape=[jax.ShapeDtypeStruct((b, n, U_MAIN), BF16),
                   jax.ShapeDtypeStruct((b, n, U_SMALL), F32)],
        scratch_shapes=[pltpu.VMEM((tm, d), BF16)],
        compiler_params=_cparams(("parallel", "parallel", "arbitrary")),
        name="in_proj",
    )(x, scale, shift, g, w_main, w_small)


_NT = (((1,), (1,)), ((), ()))


def _qkv_kernel(*refs, rope, with_q):
    if rope:
        (cq_ref, ckv_ref, kr_ref, gq_ref, gkv_ref, wq_ref, wk_ref, wv_ref,
         cost_ref, sint_ref, cos_ref, sin_ref, q_ref, k_ref, v_ref) = refs
    else:
        ckv_ref, kr_ref, gkv_ref, wk_ref, wv_ref, k_ref, v_ref = refs
    ckvn = (_rms(ckv_ref[...].astype(F32)) * gkv_ref[...]).astype(BF16)
    kr = kr_ref[...]
    kpe = kr[:, :QK_ROPE]
    if rope:
        kpe = kpe * cos_ref[...] + kr[:, QK_ROPE:] * sin_ref[...]
    kpe = kpe.astype(BF16)
    if with_q:
        cqn = (_rms(cq_ref[...].astype(F32)) * gq_ref[...]).astype(BF16)
        cost = cost_ref[...]
        sint = sint_ref[...]

    def head(h, carry):
        if with_q:
            qt = lax.dot_general(wq_ref[h], cqn, _NT, preferred_element_type=F32)
            qp = qt[QK_NOPE:]
            qsw = jnp.concatenate([qp[16:32], qp[0:16], qp[48:64], qp[32:48]], axis=0)
            qp = qp * cost + qsw * sint
            q_ref[h, :QK_NOPE, :] = (qt[:QK_NOPE] * (ATTN_SCALE * LOG2E)).astype(BF16)
            q_ref[h, QK_NOPE:, :] = (qp * (ATTN_SCALE * LOG2E)).astype(BF16)
        kn = jnp.dot(ckvn, wk_ref[h], preferred_element_type=F32)
        k_ref[h, :, :QK_NOPE] = kn.astype(BF16)
        k_ref[h, :, QK_NOPE:] = kpe
        vt = lax.dot_general(wv_ref[h], ckvn, _NT, preferred_element_type=F32)
        v_ref[h] = vt.astype(BF16)
        return carry

    lax.fori_loop(0, MLA_HEADS, head, 0)


def _qkv(u_main, u_small, g_q, g_kv, wq_t, wk, wv_t, tables, tn):
    b, n, _ = u_main.shape
    nt = n // tn
    h = MLA_HEADS
    rope = tables is not None
    in_specs = []
    args = []
    if rope:
        in_specs.append(pl.BlockSpec((None, tn, Q_LORA), lambda bi, i: (bi, i, U_CQ // Q_LORA)))
        args.append(u_main)
    in_specs += [pl.BlockSpec((None, tn, KV_LORA), lambda bi, i: (bi, i, U_CKV // KV_LORA)),
                 pl.BlockSpec((None, tn, 128), lambda bi, i: (bi, i, 1))]
    args += [u_main, u_small]
    if rope:
        in_specs.append(pl.BlockSpec((1, Q_LORA), lambda bi, i: (0, 0)))
        args.append(g_q)
    in_specs.append(pl.BlockSpec((1, KV_LORA), lambda bi, i: (0, 0)))
    args.append(g_kv)
    if rope:
        in_specs.append(pl.BlockSpec((h, QK_DIM, Q_LORA), lambda bi, i: (0, 0, 0)))
        args.append(wq_t)
    in_specs += [pl.BlockSpec((h, KV_LORA, QK_NOPE), lambda bi, i: (0, 0, 0)),
                 pl.BlockSpec((h, V_HEAD, KV_LORA), lambda bi, i: (0, 0, 0))]
    args += [wk, wv_t]
    out_specs = []
    out_shape = []
    if rope:
        cos_t, sin_t, cos, sin = tables
        in_specs += [pl.BlockSpec((QK_ROPE, tn), lambda bi, i: (0, i)),
                     pl.BlockSpec((QK_ROPE, tn), lambda bi, i: (0, i)),
                     pl.BlockSpec((tn, QK_ROPE), lambda bi, i: (i, 0)),
                     pl.BlockSpec((tn, QK_ROPE), lambda bi, i: (i, 0))]
        args += [cos_t, sin_t, cos, sin]
        out_specs.append(pl.BlockSpec((None, None, h, QK_DIM, tn), lambda bi, i: (bi, i, 0, 0, 0)))
        out_shape.append(jax.ShapeDtypeStruct((b, nt, h, QK_DIM, tn), BF16))
    out_specs += [pl.BlockSpec((None, None, h, tn, QK_DIM), lambda bi, i: (bi, i, 0, 0, 0)),
                  pl.BlockSpec((None, None, h, V_HEAD, tn), lambda bi, i: (bi, i, 0, 0, 0))]
    out_shape += [jax.ShapeDtypeStruct((b, nt, h, tn, QK_DIM), BF16),
                  jax.ShapeDtypeStruct((b, nt, h, V_HEAD, tn), BF16)]
    return pl.pallas_call(
        functools.partial(_qkv_kernel, rope=rope, with_q=rope),
        grid=(b, nt),
        in_specs=in_specs,
        out_specs=out_specs,
        out_shape=out_shape,
        compiler_params=_cparams(("parallel", "parallel")),
        name="qkv_rope" if rope else "kv_ctx",
    )(*args)


def _attn_kernel(q_ref, kl_ref, kc_ref, vl_ref, vc_ref, o_ref, sa_ref, sb_ref):
    qt = q_ref[...]
    tq = qt.shape[1]
    nt = kl_ref.shape[0]

    def scores(kb):
        s = jnp.dot(kb, qt, preferred_element_type=F32)
        return s, jnp.max(s, axis=0, keepdims=True)

    def softmax_pv(s, smax, vb, carry):
        m, l, acc = carry
        m_new = jnp.maximum(m, smax)
        alpha = jnp.exp2(m - m_new)
        p = jnp.exp2(s - m_new)
        l = alpha * l + jnp.sum(p, axis=0, keepdims=True)
        acc = alpha * acc + jnp.dot(vb, p.astype(BF16), preferred_element_type=F32)
        return m_new, l, acc

    init = (jnp.full((1, tq), -jnp.inf, F32), jnp.zeros((1, tq), F32), jnp.zeros((V_HEAD, tq), F32))
    sa_ref[...], ma = scores(kl_ref[0])
    sc, mc = scores(kc_ref[0])
    carry = softmax_pv(sc, mc, vc_ref[0], init)

    def pair(j, c):
        ma, c = c[0], c[1:]
        sb_ref[...], mb = scores(kl_ref[2 * j + 1])
        c = softmax_pv(sa_ref[...], ma, vl_ref[2 * j], c)
        sa_ref[...], ma = scores(kl_ref[jnp.minimum(2 * j + 2, nt - 1)])
        return (ma,) + softmax_pv(sb_ref[...], mb, vl_ref[2 * j + 1], c)

    pairs = nt // 2
    unroll = next(u for u in (8, 4, 2, 1) if pairs % u == 0)
    _, _, l, acc = lax.fori_loop(0, pairs, pair, (ma,) + carry, unroll=unroll)
    o_ref[...] = jnp.transpose(acc / l).astype(BF16)


def _attention(q_t, k_l, v_l, k_c, v_c, tq):
    b, nt, h, _, tn = q_t.shape
    n = nt * tn
    per = tn // tq
    tk = k_l.shape[3]
    tkc = k_c.shape[3]
    assert nt % 2 == 0, "key chunks are processed in pairs"
    return pl.pallas_call(
        _attn_kernel,
        grid=(b, h, n // tq),
        in_specs=[pl.BlockSpec((None, None, None, QK_DIM, tq), lambda bi, hi, qi: (bi, qi // per, hi, 0, qi % per)),
                  pl.BlockSpec((None, nt, None, tk, QK_DIM), lambda bi, hi, qi: (bi, 0, hi, 0, 0)),
                  pl.BlockSpec((None, 1, None, tkc, QK_DIM), lambda bi, hi, qi: (bi, 0, hi, 0, 0)),
                  pl.BlockSpec((None, nt, None, V_HEAD, tk), lambda bi, hi, qi: (bi, 0, hi, 0, 0)),
                  pl.BlockSpec((None, 1, None, V_HEAD, tkc), lambda bi, hi, qi: (bi, 0, hi, 0, 0))],
        out_specs=pl.BlockSpec((None, tq, V_HEAD), lambda bi, hi, qi: (bi, qi, hi)),
        out_shape=jax.ShapeDtypeStruct((b, n, h * V_HEAD), BF16),
        scratch_shapes=[pltpu.VMEM((tk, tq), F32), pltpu.VMEM((tk, tq), F32)],
        compiler_params=_cparams(("parallel", "parallel", "arbitrary")),
        name="attention",
    )(q_t, k_l, k_c, v_l, v_c)


_HALO = 16


_XBC_WIDTH = SSD_INNER + 2 * SSD_GROUPS * SSD_STATE
_CONV_COLS = 512


def _conv_kernel(p_ref, x_ref, n_ref, w_ref, b_ref, xg_ref, bg_ref, cg_ref, buf):
    i = pl.program_id(1)
    t = x_ref.shape[0]
    buf[0:_HALO, :] = jnp.where(i > 0, p_ref[...].astype(F32), 0.0)
    buf[_HALO:_HALO + t, :] = x_ref[...].astype(F32)
    buf[_HALO + t:, :] = jnp.where(i < pl.num_programs(1) - 1, n_ref[...].astype(F32), 0.0)
    for cb in range(_XBC_WIDTH // _CONV_COLS):
        cols = slice(cb * _CONV_COLS, (cb + 1) * _CONV_COLS)
        acc = b_ref[:, cols] + w_ref[0:1, cols] * buf[_HALO - 2:_HALO - 2 + t, cols]
        for k in range(1, SSD_CONV):
            acc = acc + w_ref[k:k + 1, cols] * buf[_HALO - 2 + k:_HALO - 2 + k + t, cols]
        y = _silu(acc).astype(BF16)
        if cb < SSD_GROUPS:
            xg_ref[cb] = y
        else:
            for q in range(_CONV_COLS // SSD_STATE):
                g = (cb - SSD_GROUPS) * (_CONV_COLS // SSD_STATE) + q
                piece = y[:, q * SSD_STATE:(q + 1) * SSD_STATE]
                if g < SSD_GROUPS:
                    bg_ref[g] = piece
                else:
                    cg_ref[g - SSD_GROUPS] = piece


def _conv(u_main, conv_w, conv_b, tc):
    b, n, _ = u_main.shape
    nh = n // _HALO
    per = tc // _HALO
    c0 = U_XBC // _XBC_WIDTH
    assert c0 * _XBC_WIDTH == U_XBC
    g = SSD_GROUPS
    return pl.pallas_call(
        _conv_kernel,
        grid=(b, n // tc),
        in_specs=[pl.BlockSpec((None, _HALO, _XBC_WIDTH), lambda bi, i: (bi, jnp.maximum(i * per - 1, 0), c0)),
                  pl.BlockSpec((None, tc, _XBC_WIDTH), lambda bi, i: (bi, i, c0)),
                  pl.BlockSpec((None, _HALO, _XBC_WIDTH), lambda bi, i: (bi, jnp.minimum((i + 1) * per, nh - 1), c0)),
                  pl.BlockSpec((SSD_CONV, _XBC_WIDTH), lambda bi, i: (0, 0)),
                  pl.BlockSpec((1, _XBC_WIDTH), lambda bi, i: (0, 0))],
        out_specs=[pl.BlockSpec((None, g, tc, GROUP_WIDTH), lambda bi, i: (bi, 0, i, 0)),
                   pl.BlockSpec((None, g, tc, SSD_STATE), lambda bi, i: (bi, 0, i, 0)),
                   pl.BlockSpec((None, g, tc, SSD_STATE), lambda bi, i: (bi, 0, i, 0))],
        out_shape=[jax.ShapeDtypeStruct((b, g, n, GROUP_WIDTH), BF16),
                   jax.ShapeDtypeStruct((b, g, n, SSD_STATE), BF16),
                   jax.ShapeDtypeStruct((b, g, n, SSD_STATE), BF16)],
        scratch_shapes=[pltpu.VMEM((tc + 2 * _HALO, _XBC_WIDTH), F32)],
        compiler_params=_cparams(("parallel", "arbitrary")),
        name="conv",
    )(u_main, u_main, u_main, conv_w, conv_b.reshape(1, -1))


def _dt_kernel(dt_ref, bias_ref, a_ref, dtb_ref, dob_ref, dsb_ref, cst_ref, dc_ref):
    q = dt_ref.shape[0]
    dt = jax.nn.softplus(dt_ref[...] + bias_ref[...])
    a = dt * a_ref[...]
    row = lax.broadcasted_iota(jnp.int32, a.shape, 0)
    cs = a
    s = 1
    while s < q:
        cs = cs + jnp.where(row >= s, pltpu.roll(cs, s, 0), 0.0)
        s *= 2
    total = cs[q - 1:q, :]
    lane = lax.broadcasted_iota(jnp.int32, a.shape, 1)
    cs = jnp.where(lane < SSD_HEADS, cs, total - cs + a)
    dtb_ref[...] = dt.astype(BF16)
    dob_ref[...] = jnp.exp(cs).astype(BF16)
    dsb_ref[...] = jnp.exp(total - cs).astype(BF16)
    cst_ref[...] = jnp.transpose(cs)
    dc_ref[...] = jnp.broadcast_to(jnp.exp(total), dc_ref.shape)


def _dt_prep(u_small, dt_bias, a_coef):
    b, n, _ = u_small.shape
    q = SSD_CHUNK
    nc = n // q
    vec = lambda: pl.BlockSpec((1, 128), lambda bi, c: (0, 0))
    tok = lambda: pl.BlockSpec((None, q, 128), lambda bi, c: (bi, c, 0))
    return pl.pallas_call(
        _dt_kernel,
        grid=(b, nc),
        in_specs=[tok(), vec(), vec()],
        out_specs=[tok(), tok(), tok(),
                   pl.BlockSpec((None, None, 128, q), lambda bi, c: (bi, c, 0, 0)),
                   pl.BlockSpec((None, None, 8, 128), lambda bi, c: (bi, c, 0, 0))],
        out_shape=[jax.ShapeDtypeStruct((b, n, 128), BF16)] * 3
        + [jax.ShapeDtypeStruct((b, nc, 128, q), F32), jax.ShapeDtypeStruct((b, nc, 8, 128), F32)],
        compiler_params=_cparams(("parallel", "parallel")),
        name="dt_prep",
    )(u_small, dt_bias, a_coef)


_TN = (((0,), (0,)), ((), ()))


def _ssd_kernel(x_ref, b_ref, c_ref, dt_ref, do_ref, ds_ref, cst_ref, dc_ref, e_ref, init_ref,
                y_ref, fin_ref, st_ref, *, direction, cpb):
    q = SSD_CHUNK
    step = pl.program_id(1)

    @pl.when(step == 0)
    def _():
        st_ref[...] = init_ref[...]

    ri = lax.broadcasted_iota(jnp.int32, (q, q), 0)
    ci = lax.broadcasted_iota(jnp.int32, (q, q), 1)
    mask = (ri >= ci) if direction == 0 else (ri <= ci)
    lane = lax.broadcasted_iota(jnp.int32, (q, 2 * SSD_HEADDIM), 1)
    left = lane < SSD_HEADDIM

    def chunk_body(cc, carry):
        c = cc if direction == 0 else cpb - 1 - cc
        rows = pl.ds(pl.multiple_of(c * q, q), q)
        dtb = dt_ref[rows, :]
        dob = do_ref[rows, :]
        dsb = ds_ref[rows, :]

        def group_body(g, carry2):
            eg = e_ref[g]
            dtx = jnp.dot(dtb, eg, preferred_element_type=F32)
            dox = jnp.dot(dob, eg, preferred_element_type=F32)
            dsx = jnp.dot(dsb, eg, preferred_element_type=F32)
            xdt = x_ref[g, rows, :].astype(F32) * dtx
            xdt_b = xdt.astype(BF16)
            xds_b = (xdt * dsx).astype(BF16)
            bg = b_ref[g, rows, :]
            cg = c_ref[g, rows, :]
            cb = lax.dot_general(cg, bg, _NT, preferred_element_type=F32)
            st = st_ref[g]
            y_off = jnp.dot(cg, st.astype(BF16), preferred_element_type=F32) * dox
            g8 = cst_ref[c, pl.ds(pl.multiple_of(direction * SSD_HEADS + g * HEADS_PER_GROUP, 8), 8), :]
            g8t = jnp.transpose(g8)
            pieces = []
            for pr in range(HEADS_PER_GROUP // 2):
                pair = xdt_b[:, pr * 128:(pr + 1) * 128]
                acc = None
                for half in range(2):
                    r = 2 * pr + half
                    seg = g8t[:, r:r + 1] - g8[r:r + 1, :]
                    m = (cb * jnp.exp(jnp.where(mask, seg, -jnp.inf))).astype(BF16)
                    xin = jnp.where(left if half == 0 else jnp.logical_not(left), pair, jnp.zeros_like(pair))
                    d = jnp.dot(m, xin, preferred_element_type=F32)
                    acc = d if acc is None else acc + d
                pieces.append(acc)
            y = jnp.concatenate(pieces, axis=1) + y_off
            y_ref[g, rows, :] = y.astype(BF16)
            st_new = lax.dot_general(bg, xds_b, _TN, preferred_element_type=F32)
            st_ref[g] = st * dc_ref[c, pl.ds(g, 1), :] + st_new
            return carry2

        lax.fori_loop(0, SSD_GROUPS, group_body, 0, unroll=8)
        return carry

    lax.fori_loop(0, cpb, chunk_body, 0)

    @pl.when(step == pl.num_programs(1) - 1)
    def _():
        fin_ref[...] = st_ref[...]


def _ssd(xg, bg, cg, dtb, dob, dsb, cst, dcx, e_mat, init, direction, cpb):
    b, g, n, _ = xg.shape
    q = SSD_CHUNK
    t = q * cpb
    nt = n // t
    ti = (lambda i: i) if direction == 0 else (lambda i: nt - 1 - i)
    grp = lambda w: pl.BlockSpec((None, g, t, w), lambda bi, i: (bi, 0, ti(i), 0))
    tok = lambda: pl.BlockSpec((None, t, 128), lambda bi, i: (bi, ti(i), 0))
    return pl.pallas_call(
        functools.partial(_ssd_kernel, direction=direction, cpb=cpb),
        grid=(b, nt),
        in_specs=[grp(GROUP_WIDTH), grp(SSD_STATE), grp(SSD_STATE), tok(), tok(), tok(),
                  pl.BlockSpec((None, cpb, 128, q), lambda bi, i: (bi, ti(i), 0, 0)),
                  pl.BlockSpec((None, cpb, None, g, GROUP_WIDTH), lambda bi, i: (bi, ti(i), direction, 0, 0)),
                  pl.BlockSpec((None, g, 128, GROUP_WIDTH), lambda bi, i: (direction, 0, 0, 0)),
                  pl.BlockSpec((None, g, SSD_STATE, GROUP_WIDTH), lambda bi, i: (bi, 0, 0, 0))],
        out_specs=[grp(GROUP_WIDTH),
                   pl.BlockSpec((None, g, SSD_STATE, GROUP_WIDTH), lambda bi, i: (bi, 0, 0, 0))],
        out_shape=[jax.ShapeDtypeStruct((b, g, n, GROUP_WIDTH), BF16),
                   jax.ShapeDtypeStruct((b, g, SSD_STATE, GROUP_WIDTH), F32)],
        scratch_shapes=[pltpu.VMEM((g, SSD_STATE, GROUP_WIDTH), F32)],
        compiler_params=_cparams(("parallel", "arbitrary")),
        name="ssd_fwd" if direction == 0 else "ssd_bwd",
    )(xg, bg, cg, dtb, dob, dsb, cst, dcx, e_mat, init)


def _ssd_out_kernel(x_ref, yf_ref, yb_ref, z0_ref, z1_ref, d_ref, g_ref, w_ref, o_ref):
    ys = []
    ss = None
    half = SSD_GROUPS // 2
    for g in range(SSD_GROUPS):
        cols = slice(g * GROUP_WIDTH, (g + 1) * GROUP_WIDTH)
        z_ref = z0_ref if g < half else z1_ref
        zcols = slice((g % half) * GROUP_WIDTH, (g % half + 1) * GROUP_WIDTH)
        y = d_ref[:, cols] * x_ref[g].astype(F32) + yf_ref[g].astype(F32) + yb_ref[g].astype(F32)
        y = y * _silu(z_ref[:, zcols].astype(F32))
        sq = jnp.sum(y * y, axis=-1, keepdims=True)
        ss = sq if ss is None else ss + sq
        ys.append(y)
    inv = lax.rsqrt(ss * (1.0 / SSD_INNER) + NORM_EPS)
    acc = None
    for g in range(SSD_GROUPS):
        cols = slice(g * GROUP_WIDTH, (g + 1) * GROUP_WIDTH)
        yn = (ys[g] * inv * g_ref[:, cols]).astype(BF16)
        d = jnp.dot(yn, w_ref[cols, :], preferred_element_type=F32)
        acc = d if acc is None else acc + d
    o_ref[...] = acc.astype(BF16)


def _ssd_out(xg, yf, yb, u_main, d_exp, g_ssd, w, tm):
    b, g, n, gw = xg.shape
    dm = w.shape[1]
    grp = lambda: pl.BlockSpec((None, g, tm, gw), lambda bi, i: (bi, 0, i, 0))
    return pl.pallas_call(
        _ssd_out_kernel,
        grid=(b, n // tm),
        in_specs=[grp(), grp(), grp(),
                  pl.BlockSpec((None, tm, HALF_INNER), lambda bi, i: (bi, i, U_Z // HALF_INNER)),
                  pl.BlockSpec((None, tm, HALF_INNER), lambda bi, i: (bi, i, U_Z // HALF_INNER + 1)),
                  pl.BlockSpec((1, SSD_INNER), lambda bi, i: (0, 0)),
                  pl.BlockSpec((1, SSD_INNER), lambda bi, i: (0, 0)),
                  pl.BlockSpec((SSD_INNER, dm), lambda bi, i: (0, 0))],
        out_specs=pl.BlockSpec((None, tm, dm), lambda bi, i: (bi, i, 0)),
        out_shape=jax.ShapeDtypeStruct((b, n, dm), BF16),
        compiler_params=_cparams(("parallel", "parallel")),
        name="ssd_out",
    )(xg, yf, yb, u_main, u_main, d_exp, g_ssd, w)


def _pack_pairs(x):
    k = x.shape[1] // 2
    u = lax.bitcast_convert_type(x.astype(BF16).astype(F32), U32)
    return (u[:, k:] & jnp.uint32(0xFFFF0000)) | (u[:, :k] >> 16)


def _unpack_pairs(u):
    lo = lax.bitcast_convert_type(u << 16, F32)
    hi = lax.bitcast_convert_type(u & jnp.uint32(0xFFFF0000), F32)
    return jnp.concatenate([lo, hi], axis=1)


_ROUTER_LANES = 128


def _route(lg):
    lane = lax.broadcasted_iota(jnp.int32, lg.shape, 1).astype(F32)
    big = float(_ROUTER_LANES)

    def first_max(v):
        mx = jnp.max(v, axis=-1, keepdims=True)
        return mx, jnp.min(jnp.where(v == mx, lane, big), axis=-1, keepdims=True)

    gl = jnp.where(lane < N_GROUPS_E, lg, -jnp.inf)
    gmax, g_sel = first_max(gl)
    g_top = 1.0 / jnp.sum(jnp.exp(gl - gmax), axis=-1, keepdims=True)
    lo = N_GROUPS_E + EXPERTS_PER_GROUP * g_sel
    el = jnp.where((lane >= lo) & (lane < lo + EXPERTS_PER_GROUP), lg, -jnp.inf)
    e1, i1 = first_max(el)
    e2, i2 = first_max(jnp.where(lane == i1, -jnp.inf, el))
    r = jnp.exp(e2 - e1)
    w1 = g_top / (1.0 + r)
    w2 = w1 * r
    ids = jnp.where(lane == 0.0, i1 - N_GROUPS_E, jnp.where(lane == 1.0, i2 - N_GROUPS_E, 0.0)).astype(jnp.int32)
    wts = jnp.where(lane == 0.0, w1, jnp.where(lane == 1.0, w2, 0.0))
    return ids, wts


def _merge_kernel(s_ref, o_ref, gs_ref, gm_ref, x_ref, ga_ref, shf_ref, scf_ref, gpost_ref, gpre_ref,
                  wo_ref, wout_ref, wrh_ref, wrl_ref, br_ref, x1_ref, hp_ref, id_ref, wt_ref):
    y_mla = jnp.dot(o_ref[...], wo_ref[...], preferred_element_type=F32)
    mix = (jax.nn.sigmoid(gs_ref[...].astype(F32)) * s_ref[...].astype(F32)
           + jax.nn.sigmoid(gm_ref[...].astype(F32)) * y_mla)
    ym = jnp.dot(mix.astype(BF16), wout_ref[...], preferred_element_type=F32)
    x1 = x_ref[...] + ga_ref[...] * (_rms(ym) * gpost_ref[...])
    x1_ref[...] = x1
    h2 = _rms(x1) * (gpre_ref[...] * (1.0 + scf_ref[...])) + shf_ref[...]
    hp_ref[...] = _pack_pairs(h2)
    hh = h2.astype(BF16)
    hl = (h2 - hh.astype(F32)).astype(BF16)
    lg = (jnp.dot(hh, wrh_ref[...], preferred_element_type=F32)
          + jnp.dot(hh, wrl_ref[...], preferred_element_type=F32)
          + jnp.dot(hl, wrh_ref[...], preferred_element_type=F32))
    ids, wts = _route(lg + br_ref[...])
    id_ref[...] = ids
    wt_ref[...] = wts


def _merge(s, o, u_main, x, g_a, sh_f, sc_f, g_post, g_pre, w_o, w_out, wr_hi, wr_lo, b_r, tm):
    b, n, d = x.shape
    tokd = lambda: pl.BlockSpec((None, tm, d), lambda bi, i: (bi, i, 0))
    modv = lambda: pl.BlockSpec((None, 1, d), lambda bi, i: (bi, 0, 0))
    vec = lambda w: pl.BlockSpec((1, w), lambda bi, i: (0, 0))
    mat = lambda r, c: pl.BlockSpec((r, c), lambda bi, i: (0, 0))
    return pl.pallas_call(
        _merge_kernel,
        grid=(b, n // tm),
        in_specs=[tokd(), tokd(),
                  pl.BlockSpec((None, tm, d), lambda bi, i: (bi, i, U_GATE // d)),
                  pl.BlockSpec((None, tm, d), lambda bi, i: (bi, i, U_GATE // d + 1)),
                  tokd(), modv(), modv(), modv(), vec(d), vec(d),
                  mat(d, d), mat(d, d), mat(d, _ROUTER_LANES), mat(d, _ROUTER_LANES), vec(_ROUTER_LANES)],
        out_specs=[tokd(),
                   pl.BlockSpec((None, tm, d // 2), lambda bi, i: (bi, i, 0)),
                   pl.BlockSpec((None, tm, _ROUTER_LANES), lambda bi, i: (bi, i, 0)),
                   pl.BlockSpec((None, tm, _ROUTER_LANES), lambda bi, i: (bi, i, 0))],
        out_shape=[jax.ShapeDtypeStruct((b, n, d), F32),
                   jax.ShapeDtypeStruct((b, n, d // 2), U32),
                   jax.ShapeDtypeStruct((b, n, _ROUTER_LANES), jnp.int32),
                   jax.ShapeDtypeStruct((b, n, _ROUTER_LANES), F32)],
        compiler_params=_cparams(("parallel", "parallel")),
        name="merge",
    )(s, o, u_main, u_main, x, g_a, sh_f, sc_f, g_post, g_pre, w_o, w_out, wr_hi, wr_lo, b_r)


_GATHER_UNROLL = 8


def _row_copy(src_ref, dst_ref, sem, src_row, dst_row):
    return pltpu.make_async_copy(src_ref.at[pl.ds(src_row, 1), :], dst_ref.at[pl.ds(dst_row, 1), :], sem)


def _gather_start(idx_ref, base, src_ref, dst_ref, sem):
    def issue(r, carry):
        _row_copy(src_ref, dst_ref, sem, idx_ref[base + r], r).start()
        return carry

    lax.fori_loop(0, dst_ref.shape[0], issue, 0, unroll=_GATHER_UNROLL)


def _gather_wait(src_ref, dst_ref, sem):
    def drain(r, carry):
        _row_copy(src_ref, dst_ref, sem, 0, r).wait()
        return carry

    lax.fori_loop(0, dst_ref.shape[0], drain, 0, unroll=_GATHER_UNROLL)


def _expert_kernel(be_ref, nu_ref, src_ref, tid_ref, h_ref, wg_ref, wu_ref, wd_ref, y_ref, x_buf, sem):
    i = pl.program_id(0)
    n_used = nu_ref[0]
    slot = i % 2

    def start_block(blk, dst_slot):
        base = src_ref[blk]
        for r in range(MOE_BLOCK):
            _row_copy(h_ref, x_buf.at[dst_slot], sem.at[dst_slot], tid_ref[base + r], r).start()

    def wait_block(dst_slot):
        for r in range(MOE_BLOCK):
            _row_copy(h_ref, x_buf.at[dst_slot], sem.at[dst_slot], 0, r).wait()

    @pl.when(i == 0)
    def _():
        start_block(0, 0)

    @pl.when(i < n_used)
    def _():
        wait_block(slot)
        start_block(i + 1, 1 - slot)
        x = _unpack_pairs(x_buf[slot]).astype(BF16)
        hg = jnp.dot(x, wg_ref[...], preferred_element_type=F32)
        hu = jnp.dot(x, wu_ref[...], preferred_element_type=F32)
        a = (_silu(hg) * hu).astype(BF16)
        y_ref[...] = _pack_pairs(jnp.dot(a, wd_ref[...], preferred_element_type=F32))

    @pl.when(i == n_used)
    def _():
        wait_block(slot)

    @pl.when(i >= n_used)
    def _():
        y_ref[...] = jnp.zeros(y_ref.shape, y_ref.dtype)


def _experts(h_pack, tid_sorted, blk_e, blk_src, n_used, w_gate, w_up, w_down):
    nb = blk_e.shape[0]
    dh = h_pack.shape[1]
    ne, d, ff = w_gate.shape
    wspec = lambda r, c: pl.BlockSpec((None, r, c), lambda i, be, nu, src, tid: (be[i], 0, 0))
    return pl.pallas_call(
        _expert_kernel,
        grid_spec=pltpu.PrefetchScalarGridSpec(
            num_scalar_prefetch=4,
            grid=(nb,),
            in_specs=[pl.BlockSpec(memory_space=pl.ANY), wspec(d, ff), wspec(d, ff), wspec(ff, d)],
            out_specs=pl.BlockSpec((MOE_BLOCK, dh), lambda i, be, nu, src, tid: (i, 0)),
            scratch_shapes=[pltpu.VMEM((2, MOE_BLOCK, dh), U32), pltpu.SemaphoreType.DMA((2,))]),
        out_shape=jax.ShapeDtypeStruct((nb * MOE_BLOCK, dh), U32),
        compiler_params=_cparams(("arbitrary",)),
        name="experts",
    )(blk_e, n_used, blk_src, tid_sorted, h_pack, w_gate, w_up, w_down)


def _final_kernel(p0_ref, p1_ref, y_ref, w_ref, x1_ref, gf_ref, gpost_ref, o_ref, y0_buf, y1_buf, sem0, sem1):
    tm = y0_buf.shape[0]
    base = (pl.program_id(0) * pl.num_programs(1) + pl.program_id(1)) * tm
    _gather_start(p0_ref, base, y_ref, y0_buf, sem0)
    _gather_start(p1_ref, base, y_ref, y1_buf, sem1)
    _gather_wait(y_ref, y0_buf, sem0)
    _gather_wait(y_ref, y1_buf, sem1)
    w = w_ref[...]
    f = w[:, 0:1] * _unpack_pairs(y0_buf[...]) + w[:, 1:2] * _unpack_pairs(y1_buf[...])
    o_ref[...] = x1_ref[...] + gf_ref[...] * (_rms(f) * gpost_ref[...])


def _final(y_buf, pos0, pos1, wts, x1, g_f, g_post, tm):
    b, n, d = x1.shape
    return pl.pallas_call(
        _final_kernel,
        grid_spec=pltpu.PrefetchScalarGridSpec(
            num_scalar_prefetch=2,
            grid=(b, n // tm),
            in_specs=[pl.BlockSpec(memory_space=pl.ANY),
                      pl.BlockSpec((None, tm, 128), lambda bi, i, p0, p1: (bi, i, 0)),
                      pl.BlockSpec((None, tm, d), lambda bi, i, p0, p1: (bi, i, 0)),
                      pl.BlockSpec((None, 1, d), lambda bi, i, p0, p1: (bi, 0, 0)),
                      pl.BlockSpec((1, d), lambda bi, i, p0, p1: (0, 0))],
            out_specs=pl.BlockSpec((None, tm, d), lambda bi, i, p0, p1: (bi, i, 0)),
            scratch_shapes=[pltpu.VMEM((tm, d // 2), U32), pltpu.VMEM((tm, d // 2), U32),
                            pltpu.SemaphoreType.DMA(()), pltpu.SemaphoreType.DMA(())]),
        out_shape=jax.ShapeDtypeStruct((b, n, d), F32),
        compiler_params=_cparams(("arbitrary", "arbitrary")),
        name="final",
    )(pos0, pos1, y_buf, wts, x1, g_f, g_post)


def _lookup(table, idx):
    e = jnp.arange(table.shape[0], dtype=jnp.int32)
    return jnp.sum(jnp.where(idx[:, None] == e[None, :], table[None, :], 0), axis=1)


def _dispatch_plan(expert_id):
    t = expert_id.shape[0]
    n_assign = t * TOP_K
    eid = expert_id.reshape(n_assign)
    eid_s, order = lax.sort((eid, jnp.arange(n_assign, dtype=jnp.int32)), num_keys=1)
    experts = jnp.arange(N_EXPERTS, dtype=jnp.int32)
    start = jnp.sum((eid_s[None, :] < experts[:, None]).astype(jnp.int32), axis=1)
    counts = jnp.concatenate([start[1:], jnp.full((1,), n_assign, jnp.int32)]) - start
    nblk_e = (counts + MOE_BLOCK - 1) // MOE_BLOCK
    bend = jnp.cumsum(nblk_e)
    pstart = (bend - nblk_e) * MOE_BLOCK
    n_used = bend[-1].astype(jnp.int32)
    n_blocks = n_assign // MOE_BLOCK + N_EXPERTS + 1
    blk = jnp.arange(n_blocks, dtype=jnp.int32)
    blk_e = jnp.sum((bend[None, :] <= jnp.minimum(blk, n_used - 1)[:, None]).astype(jnp.int32), axis=1)
    blk_e = jnp.clip(blk_e, 0, N_EXPERTS - 1)
    blk_src = _lookup(start - pstart, blk_e) + blk * MOE_BLOCK
    blk_src = jnp.where(blk < n_used, blk_src, 0).astype(jnp.int32)
    dest_s = jnp.arange(n_assign, dtype=jnp.int32) + _lookup(pstart - start, eid_s)
    _, pos = lax.sort((order, dest_s), num_keys=1)
    tid_sorted = jnp.concatenate([order // TOP_K, jnp.zeros((MOE_BLOCK,), jnp.int32)])
    return pos, tid_sorted, blk_e.astype(jnp.int32), blk_src, n_used.reshape(1)


def _rope_tables(n):
    rows = n // GRID_W
    row = jnp.repeat(jnp.arange(rows), GRID_W).astype(F32)
    col = jnp.tile(jnp.arange(GRID_W), rows).astype(F32)
    half = QK_ROPE // 2
    freqs = ROPE_THETA ** (-jnp.arange(0, half, 2, dtype=F32) / half)
    ar = row[:, None] * freqs
    ac = col[:, None] * freqs
    cos = jnp.concatenate([jnp.cos(ar), jnp.cos(ar), jnp.cos(ac), jnp.cos(ac)], axis=1)
    sin = jnp.concatenate([-jnp.sin(ar), jnp.sin(ar), -jnp.sin(ac), jnp.sin(ac)], axis=1)
    return cos.T, sin.T, cos, sin


def _pick(n, pref):
    return pref if n % pref == 0 else n


def kernel(x, c, ctx, c_ctx, w_ada, b_ada, g_pre_mix, g_post_mix, g_pre_ffn, g_post_ffn, w_in, g_q, w_uq,
           g_kv, w_ukv, w_o_mla, conv_w, conv_b, a_log, dt_bias, d_skip, g_ssd, w_ssd_out, w_out,
           w_router_group, b_router_group, w_router_expert, b_router_expert, w_exp_gate, w_exp_up, w_exp_down):
    assert w_ada.shape[0] == 1, "single-layer block"
    b, n, d = x.shape
    n_ctx = ctx.shape[1]
    h = MLA_HEADS

    w_in0 = w_in[0]
    o_ckv = Q_LORA
    o_kr = o_ckv + KV_LORA
    o_z = o_kr + QK_ROPE
    o_xbc = o_z + SSD_INNER
    o_dt = o_xbc + SSD_INNER + 2 * SSD_GROUPS * SSD_STATE
    o_gate = o_dt + 2 * SSD_HEADS
    w_kr = w_in0[:, o_kr:o_z]
    swap = jnp.concatenate([jnp.arange(16, 32), jnp.arange(0, 16), jnp.arange(48, 64), jnp.arange(32, 48)])
    assert d == HALF_INNER
    w_main = jnp.concatenate([w_in0[:, o_xbc:o_dt], w_in0[:, o_z:o_xbc], w_in0[:, o_gate:],
                              w_in0[:, :o_ckv], w_in0[:, o_ckv:o_kr]], axis=1).astype(BF16)
    w_small = jnp.concatenate([w_in0[:, o_dt:o_gate], w_kr, w_kr[:, swap]], axis=1).astype(BF16)
    wq_t = w_uq[0].reshape(Q_LORA, h, QK_DIM).transpose(1, 2, 0).astype(BF16)
    wkv = w_ukv[0].reshape(KV_LORA, h, QK_NOPE + V_HEAD)
    wk = wkv[:, :, :QK_NOPE].transpose(1, 0, 2).astype(BF16)
    wv_t = wkv[:, :, QK_NOPE:].transpose(1, 2, 0).astype(BF16)
    w_o = w_o_mla[0].astype(BF16)
    w_so = w_ssd_out[0].astype(BF16)
    w_mix = w_out[0].astype(BF16)
    w_r = jnp.concatenate([w_router_group[0], w_router_expert[0],
                           jnp.zeros((d, 128 - N_GROUPS_E - N_EXPERTS), F32)], axis=1)
    wr_hi = w_r.astype(BF16)
    wr_lo = (w_r - wr_hi.astype(F32)).astype(BF16)
    b_r = jnp.concatenate([b_router_group[0], b_router_expert[0],
                           jnp.zeros((128 - N_GROUPS_E - N_EXPERTS,), F32)]).reshape(1, 128)
    wg = w_exp_gate[0].astype(BF16)
    wu = w_exp_up[0].astype(BF16)
    wd = w_exp_down[0].astype(BF16)
    vec = lambda v: v.reshape(1, -1)

    c8 = jnp.concatenate([c, c_ctx[None, :], jnp.zeros((8 - b - 1, d), F32)], axis=0)
    mod = _ada(c8, w_ada[0], b_ada[0])
    sh_a, sc_a, g_a, sh_f, sc_f, g_f = [mod[:, i * d:(i + 1) * d] for i in range(N_MOD)]
    lat = lambda m: m[:b].reshape(b, 1, d)
    cmod = lambda m: jnp.broadcast_to(m[b:b + 1].reshape(1, 1, d), (b, 1, d))

    u_main, u_small = _inproj(x, lat(sc_a), lat(sh_a), vec(g_pre_mix[0]), w_main, w_small, _pick(n, 1024))
    uc_main, uc_small = _inproj(ctx, cmod(sc_a), cmod(sh_a), vec(g_pre_mix[0]), w_main, w_small, n_ctx)

    tn = _pick(n, 1024)
    q_t, k_l, v_l = _qkv(u_main, u_small, vec(g_q[0]), vec(g_kv[0]), wq_t, wk, wv_t, _rope_tables(n), tn)
    k_c, v_c = _qkv(uc_main, uc_small, None, vec(g_kv[0]), None, wk, wv_t, None, n_ctx)
    o_mla = _attention(q_t, k_l, v_l, k_c, v_c, _pick(tn, 512))

    a_coef = -jnp.exp(a_log[0].astype(F32)).reshape(1, 2 * SSD_HEADS)
    dtb_flat = dt_bias[0].astype(F32).reshape(1, 2 * SSD_HEADS)
    hh = jnp.arange(2 * SSD_HEADS)
    lane_head = jnp.arange(GROUP_WIDTH) // SSD_HEADDIM
    e_mat = (hh[None, None, :, None] == (jnp.arange(2)[:, None, None, None] * SSD_HEADS
                                        + jnp.arange(SSD_GROUPS)[None, :, None, None] * HEADS_PER_GROUP
                                        + lane_head[None, None, None, :])).astype(BF16)

    def ssd_inputs(um, us, tc):
        xg, bg, cg = _conv(um, conv_w[0], conv_b[0], tc)
        dtb, dob, dsb, cst, dc = _dt_prep(us, dtb_flat, a_coef)
        nc = dc.shape[1]
        dcx = jnp.repeat(dc[:, :, 0, :].reshape(b, nc, 2, SSD_GROUPS, HEADS_PER_GROUP), SSD_HEADDIM, axis=-1)
        return xg, bg, cg, dtb, dob, dsb, cst, dcx

    zero_state = jnp.zeros((b, SSD_GROUPS, SSD_STATE, GROUP_WIDTH), F32)
    ctx_in = ssd_inputs(uc_main, uc_small, n_ctx)
    lat_in = ssd_inputs(u_main, u_small, _pick(n, 256))
    cpb_c = n_ctx // SSD_CHUNK
    cpb = _pick(n, 512) // SSD_CHUNK
    ys = []
    for direction in range(2):
        _, s0 = _ssd(*ctx_in, e_mat, zero_state, direction, cpb_c)
        y_dir, _ = _ssd(*lat_in, e_mat, s0, direction, cpb)
        ys.append(y_dir)
    d_exp = jnp.repeat(d_skip[0].astype(F32), SSD_HEADDIM).reshape(1, SSD_INNER)
    s_l = _ssd_out(lat_in[0], ys[0], ys[1], u_main, d_exp, vec(g_ssd[0]), w_so, _pick(n, 256))

    x1, h_pack, ids, wts = _merge(s_l, o_mla, u_main, x, lat(g_a), lat(sh_f), lat(sc_f), vec(g_post_mix[0]),
                                  vec(g_pre_ffn[0]), w_o, w_mix, wr_hi, wr_lo, b_r, _pick(n, 256))

    t = b * n
    pos, tid_sorted, blk_e, blk_src, n_used = _dispatch_plan(ids.reshape(t, _ROUTER_LANES)[:, :TOP_K])
    y_buf = _experts(h_pack.reshape(t, d // 2), tid_sorted, blk_e, blk_src, n_used, wg, wu, wd)
    pos = pos.reshape(t, TOP_K)
    return _final(y_buf, pos[:, 0], pos[:, 1], wts, x1, lat(g_f), vec(g_post_ffn[0]), _pick(n, 256))
```
